```python
import math, functools
import jax, jax.numpy as jnp
from jax import lax
import numpy as np

D_MODEL = 1024
BATCH = 2
SEQ = 8192
DEPTH = 2
DEC_BATCH = 128
DEC_SEQ = 8
PAST_LEN = 2048
PAGE_SIZE = 128

A_HEADS = 4
A_DH = 64
A_DV = 2 * A_DH
LAMBDA_INIT = 0.8 - 0.6 * math.exp(-0.3 * 0)
B_HEADS = 4
B_DK = 64
B_DV = 128
CHUNK = 128
ROPE_BASE = 10000.0
C_HEADS = 16
C_DH = D_MODEL // C_HEADS
Q_BLOCK = 128
MOE_GROUPS = 4
MOE_EXPERTS = 8
MOE_TOPK = 2
MOE_FF = D_MODEL // 2
EPS = 1e-6

A_QW = A_HEADS * 2 * A_DH
A_KW = A_HEADS * 2 * A_DH
A_VW = A_HEADS * A_DV
B_QW = B_HEADS * B_DK
B_KW = B_HEADS * B_DK
B_VW = B_HEADS * B_DV
B_GW = B_VW
IN0_SIZES = (A_QW, A_KW, A_VW, B_QW, B_KW, B_VW, B_GW)
IN0 = sum(IN0_SIZES)
OUT0 = A_VW + B_VW
C_W = C_HEADS * C_DH
IN1 = 3 * C_W

kernel_name = "hybrid_diffattn_retention_stickbreak_hmoe_step"


def _rmsnorm(x, g):
    xf = x.astype(jnp.float32)
    y = xf * lax.rsqrt(jnp.mean(xf * xf, -1, keepdims=True) + EPS)
    return (y * g.astype(jnp.float32)).astype(x.dtype)


def _rms(x):
    xf = x.astype(jnp.float32)
    return xf * lax.rsqrt(jnp.mean(xf * xf, -1, keepdims=True) + EPS)


def _rope(x, pos):
    half = x.shape[-1] // 2
    inv = ROPE_BASE ** (-jnp.arange(half, dtype=jnp.float32) / half)
    ang = pos.astype(jnp.float32)[:, None] * inv[None, :]
    cos = jnp.cos(ang)[:, None, :]
    sin = jnp.sin(ang)[:, None, :]
    xf = x.astype(jnp.float32)
    x1, x2 = xf[..., :half], xf[..., half:]
    return jnp.concatenate([x1 * cos - x2 * sin, x1 * sin + x2 * cos], -1).astype(x.dtype)


def _alibi_slopes(n):
    return jnp.exp2(-8.0 * jnp.arange(1, n + 1, dtype=jnp.float32) / n)


def _sweep(fn, qs, qpos):
    T = qpos.shape[0]
    if T <= Q_BLOCK or T % Q_BLOCK:
        return fn(*qs, qpos)
    nb = T // Q_BLOCK

    def blk(a):
        return jnp.moveaxis(a.reshape(a.shape[0], nb, Q_BLOCK, *a.shape[2:]), 1, 0)

    out = lax.map(lambda xs: fn(*xs), tuple(blk(q) for q in qs) + (qpos.reshape(nb, Q_BLOCK),))
    out = jnp.moveaxis(out, 0, 1)
    return out.reshape(out.shape[0], T, *out.shape[3:])


def _diff_attn_block(q1, q2, qpos, *, k1, k2, v, kpos, lam):
    scale = A_DH ** -0.5
    rel = qpos[:, None] - kpos[None, :]
    mask = rel >= 0
    bias = -_alibi_slopes(A_HEADS)[:, None, None] * jnp.abs(rel).astype(jnp.float32)[None]

    def probs(q, k):
        s = jnp.einsum('bqhd,bkhd->bhqk', q, k, preferred_element_type=jnp.float32) * scale + bias
        return jax.nn.softmax(jnp.where(mask, s, -jnp.inf), axis=-1)

    p = probs(q1, k1) - lam * probs(q2, k2)
    return jnp.einsum('bhqk,bkhd->bqhd', p.astype(v.dtype), v)


def _stick_block(q, qpos, *, k, v, kpos):
    z = jnp.einsum('bqhd,bkhd->bhqk', q, k, preferred_element_type=jnp.float32) * (C_DH ** -0.5)
    mask = kpos[None, :] < qpos[:, None]
    log_keep = jnp.where(mask, jax.nn.log_sigmoid(-z), 0.0)
    after = lax.cumsum(log_keep, axis=3, reverse=True) - log_keep
    a = jnp.where(mask, jnp.exp(jax.nn.log_sigmoid(z) + after), 0.0)
    return jnp.einsum('bhqk,bkhd->bqhd', a.astype(v.dtype), v)


def _retention_chunk(q, k, v, s0, log_g):
    C = q.shape[1]
    n = jnp.arange(C, dtype=jnp.float32)
    dist = n[:, None] - n[None, :]
    decay = jnp.where(dist >= 0, jnp.exp(log_g[:, None, None] * jnp.maximum(dist, 0.0)), 0.0)
    inner = jnp.einsum('bqhd,bkhd->bhqk', q, k, preferred_element_type=jnp.float32) * decay
    vf = v.astype(jnp.float32)
    o = jnp.einsum('bhqk,bkhv->bqhv', inner, vf)
    q_dec = q.astype(jnp.float32) * jnp.exp(log_g[None, :] * (n[:, None] + 1.0))[None, :, :, None]
    o = o + jnp.einsum('bqhd,bhdv->bqhv', q_dec, s0)
    k_dec = k.astype(jnp.float32) * jnp.exp(log_g[None, :] * (C - 1.0 - n[:, None]))[None, :, :, None]
    s1 = jnp.exp(log_g * C)[None, :, None, None] * s0 + jnp.einsum('bkhd,bkhv->bhdv', k_dec, vf)
    return o, s1


def _retention(q, k, v, s0):
    Bn, T = q.shape[:2]
    log_g = jnp.log1p(-jnp.exp2(-5.0 - jnp.arange(B_HEADS, dtype=jnp.float32)))
    C = CHUNK if (T > CHUNK and T % CHUNK == 0) else T
    nc = T // C

    def ch(a):
        return jnp.moveaxis(a.reshape(Bn, nc, C, *a.shape[2:]), 1, 0)

    def body(s, qkv):
        o, s = _retention_chunk(qkv[0], qkv[1], qkv[2], s, log_g)
        return s, o

    s_fin, o = lax.scan(body, s0.astype(jnp.float32), (ch(q), ch(k), ch(v)))
    o = jnp.moveaxis(o, 0, 1).reshape(Bn, T, B_HEADS, B_DV)
    return o, s_fin


def _mixer_ab(h, pos, past_pos, ak_past, av_past, s_past, w_in, w_out, lam_q1, lam_k1, lam_q2, lam_k2, a_subln):
    Bn, T, _ = h.shape
    cuts = [int(c) for c in np.cumsum(IN0_SIZES)[:-1]]
    a_q, a_k, a_v, b_q, b_k, b_v, b_g = jnp.split(h @ w_in, cuts, axis=-1)
    a_q = a_q.reshape(Bn, T, A_HEADS, 2 * A_DH)
    a_k = a_k.reshape(Bn, T, A_HEADS, 2 * A_DH)
    a_v = a_v.reshape(Bn, T, A_HEADS, A_DV)
    k_all = jnp.concatenate([ak_past.astype(a_k.dtype), a_k], axis=1)
    v_all = jnp.concatenate([av_past.astype(a_v.dtype), a_v], axis=1)
    kpos = jnp.concatenate([past_pos, pos])
    lam = (jnp.exp(jnp.sum(lam_q1.astype(jnp.float32) * lam_k1.astype(jnp.float32)))
           - jnp.exp(jnp.sum(lam_q2.astype(jnp.float32) * lam_k2.astype(jnp.float32))) + LAMBDA_INIT)
    fn = functools.partial(_diff_attn_block, k1=k_all[..., :A_DH], k2=k_all[..., A_DH:], v=v_all, kpos=kpos, lam=lam)
    a_o = _sweep(fn, (a_q[..., :A_DH], a_q[..., A_DH:]), pos)
    a_o = _rmsnorm(a_o, a_subln) * (1.0 - LAMBDA_INIT)
    bq = _rope(b_q.reshape(Bn, T, B_HEADS, B_DK), pos)
    bk = _rope(b_k.reshape(Bn, T, B_HEADS, B_DK), pos) * (B_DK ** -0.5)
    bv = b_v.reshape(Bn, T, B_HEADS, B_DV)
    b_o, s_new = _retention(bq, bk, bv, s_past)
    b_o = jax.nn.silu(b_g) * _rms(b_o).reshape(Bn, T, B_VW).astype(h.dtype)
    out = jnp.concatenate([a_o.reshape(Bn, T, A_VW).astype(h.dtype), b_o], axis=-1) @ w_out
    return out, a_k, a_v, s_new


def _mixer_c(h, pos, past_pos, ck_past, cv_past, w_in, w_out):
    Bn, T, _ = h.shape
    q, k, v = jnp.split(h @ w_in, 3, axis=-1)
    q = q.reshape(Bn, T, C_HEADS, C_DH)
    k = k.reshape(Bn, T, C_HEADS, C_DH)
    v = v.reshape(Bn, T, C_HEADS, C_DH)
    k_all = jnp.concatenate([ck_past.astype(k.dtype), k], axis=1)
    v_all = jnp.concatenate([cv_past.astype(v.dtype), v], axis=1)
    kpos = jnp.concatenate([past_pos, pos])
    fn = functools.partial(_stick_block, k=k_all, v=v_all, kpos=kpos)
    o = _sweep(fn, (q,), pos)
    return o.reshape(Bn, T, C_W) @ w_out, k, v


def _hier_moe(x, wg, bg, we, be, w_gate, w_up, w_down):
    shp = x.shape
    xt = x.reshape(-1, shp[-1])
    g_logits = jnp.einsum('td,dg->tg', xt, wg, preferred_element_type=jnp.float32) + bg.astype(jnp.float32)
    g_idx = jnp.argmax(g_logits, axis=-1)
    g_w = jnp.take_along_axis(jax.nn.softmax(g_logits, axis=-1), g_idx[:, None], axis=-1)
    e_all = jnp.einsum('td,gde->tge', xt, we, preferred_element_type=jnp.float32) + be.astype(jnp.float32)
    e_logits = jnp.take_along_axis(e_all, g_idx[:, None, None], axis=1)[:, 0]
    top_v, top_i = lax.top_k(e_logits, MOE_TOPK)
    top_w = jax.nn.softmax(top_v, axis=-1) * g_w
    e_w = jnp.sum(jax.nn.one_hot(top_i, MOE_EXPERTS, dtype=jnp.float32) * top_w[..., None], axis=1)
    gate = (jax.nn.one_hot(g_idx, MOE_GROUPS, dtype=jnp.float32)[:, :, None] * e_w[:, None, :]).astype(x.dtype)
    y = jnp.zeros_like(xt)
    for g in range(MOE_GROUPS):
        hg = jax.nn.silu(jnp.einsum('td,edf->tef', xt, w_gate[g])) * jnp.einsum('td,edf->tef', xt, w_up[g])
        y = y + jnp.einsum('tef,efd->td', hg * gate[:, g, :, None], w_down[g])
    return y.reshape(shp)


def _gather_pages(cache, page_table):
    g = cache[page_table]
    return g.reshape(g.shape[0], g.shape[1] * g.shape[2], *g.shape[3:])


def _trunk(x, pos, past_pos, pasts, layers, ln_f):
    h = x
    new_states = []
    for l in range(DEPTH):
        ln_mix, mix_w, ln_ffn, moe_w = layers[l]
        hn = _rmsnorm(h, ln_mix)
        if l % 2 == 0:
            m, s1, s2, s3 = _mixer_ab(hn, pos, past_pos, *pasts[l], *mix_w)
            new_states.append((s1, s2, s3))
        else:
            m, s1, s2 = _mixer_c(hn, pos, past_pos, *pasts[l], *mix_w)
            new_states.append((s1, s2))
        h = h + m
        h = h + _hier_moe(_rmsnorm(h, ln_ffn), *moe_w)
    return _rmsnorm(h, ln_f), new_states


def setup_inputs(seed: int = 0) -> dict:
    key = jax.random.key(seed)
    keys = list(jax.random.split(key, 48))

    def nrm(shape, scale):
        return jax.random.normal(keys.pop(), shape, jnp.float32) * scale

    n_pages = PAST_LEN // PAGE_SIZE
    n_used = DEC_BATCH * n_pages
    n_pool = n_used + n_used // 4
    page_table = jax.random.permutation(keys.pop(), n_pool)[:n_used].reshape(DEC_BATCH, n_pages).astype(jnp.int32)
    D, G, E, F = D_MODEL, MOE_GROUPS, MOE_EXPERTS, MOE_FF

    def gain(n):
        return 1.0 + nrm((n,), 0.01)

    def moe():
        return (nrm((D, G), D ** -0.5), nrm((G,), 0.01), nrm((G, D, E), D ** -0.5), nrm((G, E), 0.01),
                nrm((G, E, D, F), D ** -0.5), nrm((G, E, D, F), D ** -0.5), nrm((G, E, F, D), F ** -0.5))

    m0 = moe()
    m1 = moe()
    return {
        "x_prompt": nrm((BATCH, SEQ, D), 1.0),
        "x_sample": nrm((DEC_BATCH, DEC_SEQ, D), 1.0),
        "cache_a_k": nrm((n_pool, PAGE_SIZE, A_HEADS, 2 * A_DH), 1.0),
        "cache_a_v": nrm((n_pool, PAGE_SIZE, A_HEADS, A_DV), 1.0),
        "state_b": nrm((DEC_BATCH, B_HEADS, B_DK, B_DV), 1.0),
        "cache_c_k": nrm((n_pool, PAGE_SIZE, C_HEADS, C_DH), 1.0),
        "cache_c_v": nrm((n_pool, PAGE_SIZE, C_HEADS, C_DH), 1.0),
        "page_table": page_table,
        "ln0_mix": gain(D),
        "w_in0": nrm((D, IN0), D ** -0.5),
        "lam_q1": nrm((A_DH,), 0.1),
        "lam_k1": nrm((A_DH,), 0.1),
        "lam_q2": nrm((A_DH,), 0.1),
        "lam_k2": nrm((A_DH,), 0.1),
        "a_subln": gain(A_DV),
        "w_out0": nrm((OUT0, D), OUT0 ** -0.5),
        "ln0_ffn": gain(D),
        "moe0_wg": m0[0], "moe0_bg": m0[1], "moe0_we": m0[2], "moe0_be": m0[3],
        "moe0_w_gate": m0[4], "moe0_w_up": m0[5], "moe0_w_down": m0[6],
        "ln1_mix": gain(D),
        "w_in1": nrm((D, IN1), D ** -0.5),
        "w_out1": nrm((C_W, D), C_W ** -0.5),
        "ln1_ffn": gain(D),
        "moe1_wg": m1[0], "moe1_bg": m1[1], "moe1_we": m1[2], "moe1_be": m1[3],
        "moe1_w_gate": m1[4], "moe1_w_up": m1[5], "moe1_w_down": m1[6],
        "ln_f": gain(D),
    }


def reference(x_prompt, x_sample, cache_a_k, cache_a_v, state_b, cache_c_k, cache_c_v, page_table,
              ln0_mix, w_in0, lam_q1, lam_k1, lam_q2, lam_k2, a_subln, w_out0, ln0_ffn,
              moe0_wg, moe0_bg, moe0_we, moe0_be, moe0_w_gate, moe0_w_up, moe0_w_down,
              ln1_mix, w_in1, w_out1, ln1_ffn,
              moe1_wg, moe1_bg, moe1_we, moe1_be, moe1_w_gate, moe1_w_up, moe1_w_down,
              ln_f):
    layers = (
        (ln0_mix, (w_in0, w_out0, lam_q1, lam_k1, lam_q2, lam_k2, a_subln), ln0_ffn,
         (moe0_wg, moe0_bg, moe0_we, moe0_be, moe0_w_gate, moe0_w_up, moe0_w_down)),
        (ln1_mix, (w_in1, w_out1), ln1_ffn,
         (moe1_wg, moe1_bg, moe1_we, moe1_be, moe1_w_gate, moe1_w_up, moe1_w_down)),
    )
    Bp, Tp, _ = x_prompt.shape
    dt = x_prompt.dtype
    pos_p = jnp.arange(Tp, dtype=jnp.int32)
    empty_pos = jnp.zeros((0,), jnp.int32)
    pasts_p = (
        (jnp.zeros((Bp, 0, A_HEADS, 2 * A_DH), dt), jnp.zeros((Bp, 0, A_HEADS, A_DV), dt),
         jnp.zeros((Bp, B_HEADS, B_DK, B_DV), jnp.float32)),
        (jnp.zeros((Bp, 0, C_HEADS, C_DH), dt), jnp.zeros((Bp, 0, C_HEADS, C_DH), dt)),
    )
    y_prompt, st_p = _trunk(x_prompt, pos_p, empty_pos, pasts_p, layers, ln_f)
    past_len = page_table.shape[1] * cache_a_k.shape[1]
    Ts = x_sample.shape[1]
    pos_s = past_len + jnp.arange(Ts, dtype=jnp.int32)
    past_pos = jnp.arange(past_len, dtype=jnp.int32)
    pasts_s = (
        (_gather_pages(cache_a_k, page_table), _gather_pages(cache_a_v, page_table), state_b),
        (_gather_pages(cache_c_k, page_table), _gather_pages(cache_c_v, page_table)),
    )
    y_sample, st_s = _trunk(x_sample, pos_s, past_pos, pasts_s, layers, ln_f)
    a_k_p, a_v_p, b_state_p = st_p[0]
    c_k_p, c_v_p = st_p[1]
    a_k_s, a_v_s, b_state_s = st_s[0]
    c_k_s, c_v_s = st_s[1]
    return (y_prompt, y_sample, a_k_p, a_v_p, b_state_p, c_k_p, c_v_p, a_k_s, a_v_s, b_state_s, c_k_s, c_v_s)
```

```python
import functools
import math

import jax
import jax.numpy as jnp
from jax import lax
from jax.experimental import pallas as pl
from jax.experimental.pallas import tpu as pltpu

F32 = jnp.float32
BF16 = jnp.bfloat16

D_MODEL = 1024
EPS = 1e-6
A_HEADS, A_DH, A_DV = 4, 64, 128
LAMBDA_INIT = 0.8 - 0.6 * math.exp(-0.3 * 0)
B_HEADS, B_DK, B_DV = 4, 64, 128
CHUNK = 128
ROPE_BASE = 10000.0
C_HEADS, C_DH = 16, 64
MOE_GROUPS, MOE_EXPERTS, MOE_TOPK = 4, 8, 2
MOE_FF = D_MODEL // 2
N_EXPERTS = MOE_GROUPS * MOE_EXPERTS

LANES = 128
ROW_TILE = 256
ATT_TILE_A = 256
ATT_TILE_C = 128
STICK_DEAD = -104.0
VMEM_LIMIT = 56 * 1024 * 1024


def _params(sem, vmem=VMEM_LIMIT):
    return pltpu.CompilerParams(dimension_semantics=sem, vmem_limit_bytes=vmem)


def _dot(a, b):
    return jnp.dot(a, b, preferred_element_type=F32)


def _dot_nt(a, b):
    return lax.dot_general(a, b, (((1,), (1,)), ((), ())), preferred_element_type=F32)


def _dot_tn(a, b):
    return lax.dot_general(a, b, (((0,), (0,)), ((), ())), preferred_element_type=F32)


def _rms(x):
    return x * lax.rsqrt(jnp.mean(x * x, axis=-1, keepdims=True) + EPS)


def _norm_proj_kernel(x_ref, g_ref, w_ref, o_ref):
    xn = _rms(x_ref[...]) * g_ref[...]
    o_ref[...] = _dot(xn.astype(BF16), w_ref[...])


def _norm_proj(x, g, w):
    n, d = x.shape
    nout = w.shape[1]
    return pl.pallas_call(
        _norm_proj_kernel,
        grid=(n // ROW_TILE,),
        in_specs=[
            pl.BlockSpec((ROW_TILE, d), lambda i: (i, 0)),
            pl.BlockSpec((1, d), lambda i: (0, 0)),
            pl.BlockSpec((d, nout), lambda i: (0, 0)),
        ],
        out_specs=pl.BlockSpec((ROW_TILE, nout), lambda i: (i, 0)),
        out_shape=jax.ShapeDtypeStruct((n, nout), F32),
        compiler_params=_params(("parallel",)),
        name="norm_proj",
    )(x, g.reshape(1, d), w)


def _rmsnorm_kernel(x_ref, g_ref, o_ref):
    o_ref[...] = _rms(x_ref[...]) * g_ref[...]


def _rmsnorm(x, g):
    n, d = x.shape
    return pl.pallas_call(
        _rmsnorm_kernel,
        grid=(n // ROW_TILE,),
        in_specs=[pl.BlockSpec((ROW_TILE, d), lambda i: (i, 0)), pl.BlockSpec((1, d), lambda i: (0, 0))],
        out_specs=pl.BlockSpec((ROW_TILE, d), lambda i: (i, 0)),
        out_shape=jax.ShapeDtypeStruct((n, d), F32),
        compiler_params=_params(("parallel",)),
        name="final_norm",
    )(x, g.reshape(1, d))


def _route(logits):
    lane = lax.broadcasted_iota(jnp.int32, logits.shape, 1).astype(F32)
    neg = jnp.float32(-jnp.inf)
    big = jnp.float32(LANES)
    gl = jnp.where(lane < MOE_GROUPS, logits, neg)
    gmax = jnp.max(gl, axis=-1, keepdims=True)
    g_idx = jnp.min(jnp.where(gl == gmax, lane, big), axis=-1, keepdims=True)
    g_w = 1.0 / jnp.sum(jnp.exp(gl - gmax), axis=-1, keepdims=True)
    lo = MOE_GROUPS + g_idx * MOE_EXPERTS
    el = jnp.where((lane >= lo) & (lane < lo + MOE_EXPERTS), logits, neg)
    v1 = jnp.max(el, axis=-1, keepdims=True)
    i1 = jnp.min(jnp.where(el == v1, lane, big), axis=-1, keepdims=True)
    el2 = jnp.where(lane == i1, neg, el)
    v2 = jnp.max(el2, axis=-1, keepdims=True)
    i2 = jnp.min(jnp.where(el2 == v2, lane, big), axis=-1, keepdims=True)
    e2 = jnp.exp(v2 - v1)
    w1 = g_w / (1.0 + e2)
    w2 = g_w * e2 / (1.0 + e2)
    ids = jnp.where(lane == 0, i1 - MOE_GROUPS, jnp.where(lane == 1, i2 - MOE_GROUPS, 0.0))
    wts = jnp.where(lane == 0, w1, jnp.where(lane == 1, w2, 0.0))
    return ids.astype(jnp.int32), wts


def _out_route_kernel(*refs, n_a):
    a_refs = refs[:n_a]
    w_refs = refs[n_a:2 * n_a]
    h_ref, g_ref, wr_ref, br_ref, ho_ref, xn_ref, ids_ref, wts_ref = refs[2 * n_a:]
    h = h_ref[...]
    for a_ref, w_ref in zip(a_refs, w_refs):
        h = h + _dot(a_ref[...].astype(BF16), w_ref[...])
    ho_ref[...] = h
    xn = _rms(h) * g_ref[...]
    xn_ref[...] = xn
    logits = jnp.dot(xn, wr_ref[...], preferred_element_type=F32, precision=lax.Precision.HIGHEST) + br_ref[...]
    ids, wts = _route(logits)
    ids_ref[...] = ids
    wts_ref[...] = wts


def _out_route(a_list, w_list, h, g, wr, br):
    n, d = h.shape
    n_a = len(a_list)
    row = lambda i: (i, 0)
    const = lambda i: (0, 0)
    in_specs = [pl.BlockSpec((ROW_TILE, a.shape[1]), row) for a in a_list]
    in_specs += [pl.BlockSpec(w.shape, const) for w in w_list]
    in_specs += [pl.BlockSpec((ROW_TILE, d), row), pl.BlockSpec((1, d), const),
                 pl.BlockSpec((d, LANES), const), pl.BlockSpec((1, LANES), const)]
    return pl.pallas_call(
        functools.partial(_out_route_kernel, n_a=n_a),
        grid=(n // ROW_TILE,),
        in_specs=in_specs,
        out_specs=[pl.BlockSpec((ROW_TILE, d), row), pl.BlockSpec((ROW_TILE, d), row),
                   pl.BlockSpec((ROW_TILE, LANES), row), pl.BlockSpec((ROW_TILE, LANES), row)],
        out_shape=[jax.ShapeDtypeStruct((n, d), F32), jax.ShapeDtypeStruct((n, d), F32),
                   jax.ShapeDtypeStruct((n, LANES), jnp.int32), jax.ShapeDtypeStruct((n, LANES), F32)],
        compiler_params=_params(("parallel",)),
        name="out_route",
    )(*a_list, *w_list, h, g.reshape(1, d), wr, br)


def _moe_ffn_kernel(te_ref, nv_ref, st_ref, x_hbm, wg_ref, wu_ref, wd_ref, rw_ref, o_ref, xbuf, sem):
    t = pl.program_id(0)

    @pl.when(t < nv_ref[0])
    def _():
        def issue(r, c):
            tok = st_ref[t * ROW_TILE + r]
            pltpu.make_async_copy(x_hbm.at[pl.ds(tok, 1), :], xbuf.at[pl.ds(r, 1), :], sem).start()
            return c

        lax.fori_loop(0, ROW_TILE, issue, 0)
        pltpu.make_async_copy(x_hbm.at[pl.ds(0, ROW_TILE), :], xbuf, sem).wait()
        x = xbuf[...].astype(BF16)
        hid = jax.nn.silu(_dot(x, wg_ref[0])) * _dot(x, wu_ref[0])
        o_ref[...] = _dot(hid.astype(BF16), wd_ref[0]) * rw_ref[...]

    @pl.when(t >= nv_ref[0])
    def _():
        o_ref[...] = jnp.zeros_like(o_ref)


def _moe_ffn(xn, tile_expert, n_valid, src_tok, row_w, w_gate, w_up, w_down):
    n, d = xn.shape
    p = src_tok.shape[0]
    n_tiles = p // ROW_TILE
    f = w_gate.shape[2]
    grid_spec = pltpu.PrefetchScalarGridSpec(
        num_scalar_prefetch=3,
        grid=(n_tiles,),
        in_specs=[
            pl.BlockSpec(memory_space=pl.ANY),
            pl.BlockSpec((1, d, f), lambda t, te, nv, st: (te[t], 0, 0)),
            pl.BlockSpec((1, d, f), lambda t, te, nv, st: (te[t], 0, 0)),
            pl.BlockSpec((1, f, d), lambda t, te, nv, st: (te[t], 0, 0)),
            pl.BlockSpec((ROW_TILE, 1), lambda t, te, nv, st: (t, 0)),
        ],
        out_specs=pl.BlockSpec((ROW_TILE, d), lambda t, te, nv, st: (t, 0)),
        scratch_shapes=[pltpu.VMEM((ROW_TILE, d), F32), pltpu.SemaphoreType.DMA(())],
    )
    return pl.pallas_call(
        _moe_ffn_kernel,
        grid_spec=grid_spec,
        out_shape=jax.ShapeDtypeStruct((p, d), F32),
        compiler_params=_params(("arbitrary",)),
        name="moe_ffn",
    )(tile_expert, n_valid, src_tok, xn, w_gate, w_up, w_down, row_w)


def _moe_combine_kernel(pos_ref, h_ref, y_hbm, o_ref, rbuf, sem):
    i = pl.program_id(0)

    def issue(r, c):
        for k in range(MOE_TOPK):
            pr = pos_ref[(i * ROW_TILE + r) * MOE_TOPK + k]
            pltpu.make_async_copy(y_hbm.at[pl.ds(pr, 1), :], rbuf.at[k, pl.ds(r, 1), :], sem).start()
        return c

    lax.fori_loop(0, ROW_TILE, issue, 0)
    for k in range(MOE_TOPK):
        pltpu.make_async_copy(y_hbm.at[pl.ds(0, ROW_TILE), :], rbuf.at[k], sem).wait()
    acc = h_ref[...]
    for k in range(MOE_TOPK):
        acc = acc + rbuf[k]
    o_ref[...] = acc


def _moe_combine(h, y_sorted, pos):
    n, d = h.shape
    grid_spec = pltpu.PrefetchScalarGridSpec(
        num_scalar_prefetch=1,
        grid=(n // ROW_TILE,),
        in_specs=[pl.BlockSpec((ROW_TILE, d), lambda i, pos: (i, 0)), pl.BlockSpec(memory_space=pl.ANY)],
        out_specs=pl.BlockSpec((ROW_TILE, d), lambda i, pos: (i, 0)),
        scratch_shapes=[pltpu.VMEM((MOE_TOPK, ROW_TILE, d), F32), pltpu.SemaphoreType.DMA(())],
    )
    return pl.pallas_call(
        _moe_combine_kernel,
        grid_spec=grid_spec,
        out_shape=jax.ShapeDtypeStruct((n, d), F32),
        compiler_params=_params(("arbitrary",)),
        name="moe_combine",
    )(pos, h, y_sorted)


def _dispatch_plan(ids, wts):
    n = ids.shape[0]
    a = n * MOE_TOPK
    n_tiles = a // ROW_TILE + N_EXPERTS
    e_flat = ids.reshape(a)
    w_flat = wts.reshape(a)
    order = jnp.argsort(e_flat, stable=True).astype(jnp.int32)
    sorted_e = e_flat[order]
    counts = jnp.sum((e_flat[:, None] == jnp.arange(N_EXPERTS, dtype=jnp.int32)[None, :]).astype(jnp.int32), axis=0)
    tiles_per = (counts + ROW_TILE - 1) // ROW_TILE
    tile_end = jnp.cumsum(tiles_per)
    tile_start = tile_end - tiles_per
    group_start = jnp.cumsum(counts) - counts
    rank = jnp.arange(a, dtype=jnp.int32) - group_start[sorted_e]
    pos_sorted = (tile_start[sorted_e] * ROW_TILE + rank).astype(jnp.int32)
    p = n_tiles * ROW_TILE
    src_tok = jnp.zeros((p,), jnp.int32).at[pos_sorted].set(order // MOE_TOPK)
    row_w = jnp.zeros((p,), F32).at[pos_sorted].set(w_flat[order])
    pos = jnp.zeros((a,), jnp.int32).at[order].set(pos_sorted)
    n_valid = tile_end[-1].astype(jnp.int32)
    t_idx = jnp.arange(n_tiles, dtype=jnp.int32)
    te = jnp.minimum(jnp.searchsorted(tile_end, t_idx, side="right").astype(jnp.int32), N_EXPERTS - 1)
    last = te[jnp.maximum(n_valid - 1, 0)]
    tile_expert = jnp.where(t_idx < n_valid, te, last)
    return tile_expert, n_valid.reshape(1), src_tok, row_w.reshape(p, 1), pos


def _moe(h, xn, ids, wts, w_gate, w_up, w_down):
    tile_expert, n_valid, src_tok, row_w, pos = _dispatch_plan(ids, wts)
    y_sorted = _moe_ffn(xn, tile_expert, n_valid, src_tok, row_w, w_gate, w_up, w_down)
    return _moe_combine(h, y_sorted, pos)


def _lambda_from(lam_ref):
    lp = lam_ref[...]
    s1 = jnp.sum(lp[0:1, :] * lp[1:2, :], axis=-1, keepdims=True)
    s2 = jnp.sum(lp[2:3, :] * lp[3:4, :], axis=-1, keepdims=True)
    return jnp.exp(s1) - jnp.exp(s2) + LAMBDA_INIT


def _diff_prompt_kernel(slopes_ref, q_ref, k_ref, v_ref, lam_ref, g_ref, o_ref, m_scr, l_scr, acc_scr, *, tile):
    h = pl.program_id(1)
    qi = pl.program_id(2)
    slope = slopes_ref[h]
    q = q_ref[...] * (A_DH ** -0.5)
    lane = lax.broadcasted_iota(jnp.int32, q.shape, 1)
    qs = (jnp.where(lane < A_DH, q, 0.0).astype(BF16), jnp.where(lane >= A_DH, q, 0.0).astype(BF16))
    r = lax.broadcasted_iota(jnp.int32, (tile, tile), 0)
    c = lax.broadcasted_iota(jnp.int32, (tile, tile), 1)
    bias0 = -slope * (r - c).astype(F32)
    m_scr[...] = jnp.full(m_scr.shape, -jnp.inf, F32)
    l_scr[...] = jnp.zeros(l_scr.shape, F32)
    acc_scr[...] = jnp.zeros(acc_scr.shape, F32)

    def block(j, masked):
        start = pl.multiple_of(j * tile, tile)
        k = k_ref[pl.ds(start, tile), :].astype(BF16)
        v = v_ref[pl.ds(start, tile), :].astype(BF16)
        cb = -slope * ((qi - j) * tile).astype(F32)
        for m in range(2):
            t = _dot_nt(qs[m], k) + bias0
            if masked:
                t = jnp.where(r >= c, t, -jnp.inf)
            m_old = m_scr[m]
            m_new = jnp.maximum(m_old, jnp.max(t, axis=-1, keepdims=True) + cb)
            p = jnp.exp(t + (cb - m_new))
            alpha = jnp.exp(m_old - m_new)
            l_scr[m] = alpha * l_scr[m] + jnp.sum(p, axis=-1, keepdims=True)
            acc_scr[m] = alpha * acc_scr[m] + _dot(p.astype(BF16), v)
            m_scr[m] = m_new

    def body(j, carry):
        block(j, False)
        return carry

    lax.fori_loop(0, qi, body, 0)
    block(qi, True)
    lam = _lambda_from(lam_ref)
    o = acc_scr[0] / l_scr[0] - lam * (acc_scr[1] / l_scr[1])
    o_ref[...] = _rms(o) * g_ref[...] * (1.0 - LAMBDA_INIT)


def _diff_prompt(proj, slopes, lam_pack, subln, n_batch, t_len):
    tile = min(ATT_TILE_A, t_len)
    nq = t_len // tile
    qcol, kcol, vcol = 0, A_HEADS, 2 * A_HEADS
    grid_spec = pltpu.PrefetchScalarGridSpec(
        num_scalar_prefetch=1,
        grid=(n_batch, A_HEADS, nq),
        in_specs=[
            pl.BlockSpec((tile, LANES), lambda b, h, i, s: (b * nq + i, qcol + h)),
            pl.BlockSpec((t_len, LANES), lambda b, h, i, s: (b, kcol + h)),
            pl.BlockSpec((t_len, LANES), lambda b, h, i, s: (b, vcol + h)),
            pl.BlockSpec((8, LANES), lambda b, h, i, s: (0, 0)),
            pl.BlockSpec((1, LANES), lambda b, h, i, s: (0, 0)),
        ],
        out_specs=pl.BlockSpec((tile, LANES), lambda b, h, i, s: (b * nq + i, h)),
        scratch_shapes=[pltpu.VMEM((2, tile, 1), F32), pltpu.VMEM((2, tile, 1), F32),
                        pltpu.VMEM((2, tile, LANES), F32)],
    )
    return pl.pallas_call(
        functools.partial(_diff_prompt_kernel, tile=tile),
        grid_spec=grid_spec,
        out_shape=jax.ShapeDtypeStruct((n_batch * t_len, A_HEADS * A_DV), F32),
        compiler_params=_params(("parallel", "parallel", "arbitrary")),
        name="diff_prompt",
    )(slopes, proj, proj, proj, lam_pack, subln.reshape(1, LANES))


def _diff_sample_kernel(pt_ref, slopes_ref, q_ref, kn_ref, vn_ref, kp_ref, vp_ref, lam_ref, g_ref, o_ref,
                        m_scr, l_scr, acc_scr, *, ts, page, n_pages):
    p = pl.program_id(1)
    rows = 2 * A_HEADS * ts
    width = A_HEADS * 2 * A_DH
    past = n_pages * page
    q = q_ref[...] * (A_DH ** -0.5)
    ri = lax.broadcasted_iota(jnp.int32, (rows, width), 0)
    ci = lax.broadcasted_iota(jnp.int32, (rows, width), 1)
    r_map, r_head = ri // (A_HEADS * ts), (ri // ts) % A_HEADS
    own = (ci // (2 * A_DH) == r_head) & ((ci // A_DH) % 2 == r_map)
    qb = jnp.where(own, jnp.concatenate([q] * (2 * A_HEADS), axis=0), 0.0).astype(BF16)
    slope_rows = jnp.zeros((rows, 1), F32)
    rh = (lax.broadcasted_iota(jnp.int32, (rows, 1), 0) // ts) % A_HEADS
    for hh in range(A_HEADS):
        slope_rows = jnp.where(rh == hh, slopes_ref[hh], slope_rows)
    rt = lax.broadcasted_iota(jnp.int32, (rows, 1), 0) % ts

    def update(k, v, kpos, mask):
        rel = (past + rt) - kpos
        t = _dot_nt(qb, k) - slope_rows * rel.astype(F32)
        if mask:
            t = jnp.where(rel >= 0, t, -jnp.inf)
        m_old = m_scr[...]
        m_new = jnp.maximum(m_old, jnp.max(t, axis=-1, keepdims=True))
        pr = jnp.exp(t - m_new)
        alpha = jnp.exp(m_old - m_new)
        l_scr[...] = alpha * l_scr[...] + jnp.sum(pr, axis=-1, keepdims=True)
        acc_scr[...] = alpha * acc_scr[...] + _dot(pr.astype(BF16), v)
        m_scr[...] = m_new

    @pl.when(p == 0)
    def _():
        m_scr[...] = jnp.full(m_scr.shape, -jnp.inf, F32)
        l_scr[...] = jnp.zeros(l_scr.shape, F32)
        acc_scr[...] = jnp.zeros(acc_scr.shape, F32)
        kpos = past + lax.broadcasted_iota(jnp.int32, (1, ts), 1)
        update(kn_ref[...].astype(BF16), vn_ref[...].astype(BF16), kpos, True)

    kpos = p * page + lax.broadcasted_iota(jnp.int32, (1, page), 1)
    update(kp_ref[0].astype(BF16), vp_ref[0].astype(BF16), kpos, False)

    @pl.when(p == n_pages - 1)
    def _():
        lam = _lambda_from(lam_ref)
        o = acc_scr[...] / l_scr[...]
        half = A_HEADS * ts
        o = o[:half] - lam * o[half:]
        oh = lax.broadcasted_iota(jnp.int32, o.shape, 0) // ts
        oc = lax.broadcasted_iota(jnp.int32, o.shape, 1) // A_DV
        o = jnp.where(oh == oc, o, 0.0)
        res = o[0:ts]
        for hh in range(1, A_HEADS):
            res = res + o[hh * ts:(hh + 1) * ts]
        g = g_ref[...]
        outs = [_rms(res[:, hh * A_DV:(hh + 1) * A_DV]) * g * (1.0 - LAMBDA_INIT) for hh in range(A_HEADS)]
        o_ref[...] = jnp.concatenate(outs, axis=-1)


def _diff_sample(proj, row0, cache_k, cache_v, page_table, slopes, lam_pack, subln, n_seq, ts):
    n_pages = page_table.shape[1]
    page = cache_k.shape[1]
    width = A_HEADS * 2 * A_DH
    vwidth = A_HEADS * A_DV
    rb = row0 // ts
    rows = 2 * A_HEADS * ts
    qblk, kblk, vblk = 0, 1, 2
    grid_spec = pltpu.PrefetchScalarGridSpec(
        num_scalar_prefetch=2,
        grid=(n_seq, n_pages),
        in_specs=[
            pl.BlockSpec((ts, width), lambda s, p, pt, sl: (rb + s, qblk)),
            pl.BlockSpec((ts, width), lambda s, p, pt, sl: (rb + s, kblk)),
            pl.BlockSpec((ts, vwidth), lambda s, p, pt, sl: (rb + s, vblk)),
            pl.BlockSpec((1, page, width), lambda s, p, pt, sl: (pt[s, p], 0, 0)),
            pl.BlockSpec((1, page, vwidth), lambda s, p, pt, sl: (pt[s, p], 0, 0)),
            pl.BlockSpec((8, LANES), lambda s, p, pt, sl: (0, 0)),
            pl.BlockSpec((1, LANES), lambda s, p, pt, sl: (0, 0)),
        ],
        out_specs=pl.BlockSpec((ts, vwidth), lambda s, p, pt, sl: (s, 0)),
        scratch_shapes=[pltpu.VMEM((rows, 1), F32), pltpu.VMEM((rows, 1), F32), pltpu.VMEM((rows, vwidth), F32)],
    )
    return pl.pallas_call(
        functools.partial(_diff_sample_kernel, ts=ts, page=page, n_pages=n_pages),
        grid_spec=grid_spec,
        out_shape=jax.ShapeDtypeStruct((n_seq * ts, vwidth), F32),
        compiler_params=_params(("parallel", "arbitrary")),
        name="diff_sample",
    )(page_table, slopes, proj, proj, proj, cache_k.reshape(-1, page, width), cache_v.reshape(-1, page, vwidth),
      lam_pack, subln.reshape(1, LANES))


def _retention_kernel(q_ref, k_ref, v_ref, g_ref, cos_ref, sin_ref, s0_ref, o_ref, s_out_ref, s_scr, *, chunk):
    c = pl.program_id(1)

    @pl.when(c == 0)
    def _():
        s_scr[...] = s0_ref[0]

    cos = cos_ref[...]
    sin = sin_ref[...]
    width = B_HEADS * B_DK
    lane = lax.broadcasted_iota(jnp.int32, (chunk, width), 1)
    first_half = (lane % B_DK) < (B_DK // 2)

    def rope(x):
        partner = jnp.where(first_half, pltpu.roll(x, width - B_DK // 2, 1), pltpu.roll(x, B_DK // 2, 1))
        return x * cos + partner * sin

    q = rope(q_ref[...])
    k = rope(k_ref[...]) * (B_DK ** -0.5)
    v = v_ref[...]
    g = g_ref[...]
    n_col = lax.broadcasted_iota(jnp.int32, (chunk, 1), 0).astype(F32)
    ri = lax.broadcasted_iota(jnp.int32, (chunk, chunk), 0)
    ci = lax.broadcasted_iota(jnp.int32, (chunk, chunk), 1)
    dist = (ri - ci).astype(F32)
    outs = []
    for hh in range(B_HEADS):
        log_g = math.log1p(-(2.0 ** (-5.0 - hh)))
        decay = jnp.where(ri >= ci, jnp.exp(log_g * jnp.maximum(dist, 0.0)), 0.0)
        qh = q[:, hh * B_DK:(hh + 1) * B_DK]
        kh = k[:, hh * B_DK:(hh + 1) * B_DK]
        vh = v[:, hh * B_DV:(hh + 1) * B_DV].astype(BF16)
        s_h = s_scr[hh]
        inner = _dot_nt(qh.astype(BF16), kh.astype(BF16)) * decay
        q_dec = qh * jnp.exp(log_g * (n_col + 1.0))
        o = _dot(inner.astype(BF16), vh) + _dot(q_dec.astype(BF16), s_h.astype(BF16))
        k_dec = kh * jnp.exp(log_g * (chunk - 1.0 - n_col))
        s_scr[hh] = math.exp(log_g * chunk) * s_h + _dot_tn(k_dec.astype(BF16), vh)
        gh = g[:, hh * B_DV:(hh + 1) * B_DV]
        outs.append(jax.nn.silu(gh) * _rms(o))
    o_ref[...] = jnp.concatenate(outs, axis=-1)

    @pl.when(c == pl.num_programs(1) - 1)
    def _():
        s_out_ref[0] = s_scr[...]


def _retention(proj, row0, cos_t, sin_t, s0, n_seq, t_len):
    chunk = CHUNK if (t_len > CHUNK and t_len % CHUNK == 0) else t_len
    nc = t_len // chunk
    rb = row0 // chunk
    qw, vw = B_HEADS * B_DK, B_HEADS * B_DV
    qcol, kcol, vcol, gcol = 6, 7, 4, 5
    row = lambda s, c: (rb + s * nc + c, 0)
    return pl.pallas_call(
        functools.partial(_retention_kernel, chunk=chunk),
        grid=(n_seq, nc),
        in_specs=[
            pl.BlockSpec((chunk, qw), lambda s, c: (rb + s * nc + c, qcol)),
            pl.BlockSpec((chunk, qw), lambda s, c: (rb + s * nc + c, kcol)),
            pl.BlockSpec((chunk, vw), lambda s, c: (rb + s * nc + c, vcol)),
            pl.BlockSpec((chunk, vw), lambda s, c: (rb + s * nc + c, gcol)),
            pl.BlockSpec((chunk, qw), lambda s, c: (c, 0)),
            pl.BlockSpec((chunk, qw), lambda s, c: (c, 0)),
            pl.BlockSpec((1, B_HEADS, B_DK, B_DV), lambda s, c: (s, 0, 0, 0)),
        ],
        out_specs=[pl.BlockSpec((chunk, vw), lambda s, c: (s * nc + c, 0)),
                   pl.BlockSpec((1, B_HEADS, B_DK, B_DV), lambda s, c: (s, 0, 0, 0))],
        out_shape=[jax.ShapeDtypeStruct((n_seq * t_len, vw), F32),
                   jax.ShapeDtypeStruct((n_seq, B_HEADS, B_DK, B_DV), F32)],
        scratch_shapes=[pltpu.VMEM((B_HEADS, B_DK, B_DV), F32)],
        compiler_params=_params(("parallel", "arbitrary")),
        name="retention",
    )(proj, proj, proj, proj, cos_t, sin_t, s0)


def _rope_tables(pos):
    half = B_DK // 2
    inv = ROPE_BASE ** (-jnp.arange(half, dtype=F32) / half)
    ang = pos.astype(F32)[:, None] * inv[None, :]
    cos, sin = jnp.cos(ang), jnp.sin(ang)
    cos_t = jnp.tile(jnp.concatenate([cos, cos], axis=-1), (1, B_HEADS))
    sin_t = jnp.tile(jnp.concatenate([-sin, sin], axis=-1), (1, B_HEADS))
    return cos_t, sin_t


def _stick_block(qh, k, v, carry, upper, mask):
    z = _dot_nt(qh, k)
    sp = jnp.maximum(z, 0.0) + jnp.log1p(jnp.exp(-jnp.abs(z)))
    lk = -sp
    if mask is not None:
        lk = jnp.where(mask, lk, 0.0)
    hi = lk.astype(BF16)
    lo = (lk - hi.astype(F32)).astype(BF16)
    newer = _dot(hi, upper) + _dot(lo, upper)
    a = jnp.exp((z - sp) + newer + carry)
    if mask is not None:
        a = jnp.where(mask, a, 0.0)
    return _dot(a.astype(BF16), v), carry + jnp.sum(lk, axis=-1, keepdims=True)


def _stick_prompt_kernel(q_ref, k_ref, v_ref, o_ref, carry_scr, acc_scr, *, tile):
    qi = pl.program_id(2)
    q = q_ref[...] * (C_DH ** -0.5)
    lane = lax.broadcasted_iota(jnp.int32, q.shape, 1)
    r = lax.broadcasted_iota(jnp.int32, (tile, tile), 0)
    c = lax.broadcasted_iota(jnp.int32, (tile, tile), 1)
    upper = jnp.where(r > c, 1.0, 0.0).astype(BF16)
    causal = c < r
    heads = LANES // C_DH
    for hh in range(heads):
        qh = jnp.where((lane >= hh * C_DH) & (lane < (hh + 1) * C_DH), q, 0.0).astype(BF16)

        def block(j, mask, carry):
            start = pl.multiple_of(j * tile, tile)
            k = k_ref[pl.ds(start, tile), :].astype(BF16)
            v = v_ref[pl.ds(start, tile), :].astype(BF16)
            return _stick_block(qh, k, v, carry, upper, mask)

        av, carry = block(qi, causal, jnp.zeros((tile, 1), F32))
        acc_scr[hh] = av
        carry_scr[...] = carry

        def cond(state):
            j, cmax = state
            return (j >= 0) & (cmax > STICK_DEAD)

        def body(state):
            j, _ = state
            av, carry = block(j, None, carry_scr[...])
            acc_scr[hh] += av
            carry_scr[...] = carry
            return j - 1, jnp.max(carry)

        lax.while_loop(cond, body, (qi - 1, jnp.max(carry)))
    out = acc_scr[0]
    for hh in range(1, heads):
        out = jnp.where(lane >= hh * C_DH, acc_scr[hh], out)
    o_ref[...] = out


def _stick_prompt(proj, n_batch, t_len):
    tile = min(ATT_TILE_C, t_len)
    nq = t_len // tile
    ng = C_HEADS * C_DH // LANES
    return pl.pallas_call(
        functools.partial(_stick_prompt_kernel, tile=tile),
        grid=(n_batch, ng, nq),
        in_specs=[
            pl.BlockSpec((tile, LANES), lambda b, g, i: (b * nq + i, g)),
            pl.BlockSpec((t_len, LANES), lambda b, g, i: (b, ng + g)),
            pl.BlockSpec((t_len, LANES), lambda b, g, i: (b, 2 * ng + g)),
        ],
        out_specs=pl.BlockSpec((tile, LANES), lambda b, g, i: (b * nq + i, g)),
        out_shape=jax.ShapeDtypeStruct((n_batch * t_len, C_HEADS * C_DH), F32),
        scratch_shapes=[pltpu.VMEM((tile, 1), F32), pltpu.VMEM((LANES // C_DH, tile, LANES), F32)],
        compiler_params=_params(("parallel", "parallel", "arbitrary")),
        name="stick_prompt",
    )(proj, proj, proj)


def _stick_sample_kernel(pt_ref, q_ref, kn_ref, vn_ref, kp_ref, vp_ref, o_ref, carry_scr, acc_scr, *, ts, page):
    p = pl.program_id(1)
    rows = C_HEADS * ts
    width = C_HEADS * C_DH
    q = q_ref[...] * (C_DH ** -0.5)
    ri = lax.broadcasted_iota(jnp.int32, (rows, width), 0)
    ci = lax.broadcasted_iota(jnp.int32, (rows, width), 1)
    own = (ci // C_DH) == (ri // ts)
    qb = jnp.where(own, jnp.concatenate([q] * C_HEADS, axis=0), 0.0).astype(BF16)
    r = lax.broadcasted_iota(jnp.int32, (page, page), 0)
    c = lax.broadcasted_iota(jnp.int32, (page, page), 1)
    upper = jnp.where(r > c, 1.0, 0.0).astype(BF16)

    @pl.when(p == 0)
    def _():
        rt = lax.broadcasted_iota(jnp.int32, (rows, page), 0) % ts
        kc = lax.broadcasted_iota(jnp.int32, (rows, page), 1)
        av, carry = _stick_block(qb, kn_ref[...].astype(BF16), vn_ref[...].astype(BF16),
                                 jnp.zeros((rows, 1), F32), upper, kc < rt)
        acc_scr[...] = av
        carry_scr[...] = carry

    carry = carry_scr[...]

    @pl.when(jnp.max(carry) > STICK_DEAD)
    def _():
        av, new_carry = _stick_block(qb, kp_ref[0].astype(BF16), vp_ref[0].astype(BF16), carry, upper, None)
        acc_scr[...] += av
        carry_scr[...] = new_carry

    @pl.when(p == pl.num_programs(1) - 1)
    def _():
        o = jnp.where(own, acc_scr[...], 0.0)
        res = o[0:ts]
        for hh in range(1, C_HEADS):
            res = res + o[hh * ts:(hh + 1) * ts]
        o_ref[...] = res


def _stick_sample(proj, row0, cache_k, cache_v, page_table, n_seq, ts):
    n_pages = page_table.shape[1]
    page = cache_k.shape[1]
    width = C_HEADS * C_DH
    rows = C_HEADS * ts
    rb = row0 // ts
    newest_first = lambda s, p, pt: (pt[s, n_pages - 1 - p], 0, 0)
    new_k = jnp.pad(proj[row0:, width:2 * width].reshape(n_seq, ts, width), ((0, 0), (0, page - ts), (0, 0)))
    new_v = jnp.pad(proj[row0:, 2 * width:].reshape(n_seq, ts, width), ((0, 0), (0, page - ts), (0, 0)))
    grid_spec = pltpu.PrefetchScalarGridSpec(
        num_scalar_prefetch=1,
        grid=(n_seq, n_pages),
        in_specs=[
            pl.BlockSpec((ts, width), lambda s, p, pt: (rb + s, 0)),
            pl.BlockSpec((page, width), lambda s, p, pt: (s, 0)),
            pl.BlockSpec((page, width), lambda s, p, pt: (s, 0)),
            pl.BlockSpec((1, page, width), newest_first),
            pl.BlockSpec((1, page, width), newest_first),
        ],
        out_specs=pl.BlockSpec((ts, width), lambda s, p, pt: (s, 0)),
        scratch_shapes=[pltpu.VMEM((rows, 1), F32), pltpu.VMEM((rows, width), F32)],
    )
    return pl.pallas_call(
        functools.partial(_stick_sample_kernel, ts=ts, page=page),
        grid_spec=grid_spec,
        out_shape=jax.ShapeDtypeStruct((n_seq * ts, width), F32),
        compiler_params=_params(("parallel", "arbitrary")),
        name="stick_sample",
    )(page_table, proj, new_k.reshape(n_seq * page, width), new_v.reshape(n_seq * page, width),
      cache_k.reshape(-1, page, width), cache_v.reshape(-1, page, width))


def _router_pack(wg, bg, we, be):
    d = wg.shape[0]
    we_flat = jnp.transpose(we, (1, 0, 2)).reshape(d, N_EXPERTS)
    w = jnp.concatenate([wg, we_flat], axis=1)
    b = jnp.concatenate([bg, be.reshape(N_EXPERTS)])
    pad = LANES - w.shape[1]
    return jnp.pad(w, ((0, 0), (0, pad))), jnp.pad(b, (0, pad)).reshape(1, LANES)


def _expert_weights(w_gate, w_up, w_down):
    d, f = w_gate.shape[2], w_gate.shape[3]
    return (w_gate.reshape(N_EXPERTS, d, f).astype(BF16), w_up.reshape(N_EXPERTS, d, f).astype(BF16),
            w_down.reshape(N_EXPERTS, f, d).astype(BF16))


def kernel(x_prompt, x_sample, cache_a_k, cache_a_v, state_b, cache_c_k, cache_c_v, page_table, ln0_mix, w_in0, lam_q1, lam_k1, lam_q2, lam_k2, a_subln, w_out0, ln0_ffn, moe0_wg, moe0_bg, moe0_we, moe0_be, moe0_w_gate, moe0_w_up, moe0_w_down, ln1_mix, w_in1, w_out1, ln1_ffn, moe1_wg, moe1_bg, moe1_we, moe1_be, moe1_w_gate, moe1_w_up, moe1_w_down, ln_f):
    bp, tp, d = x_prompt.shape
    db, ts, _ = x_sample.shape
    n_p, n_s = bp * tp, db * ts
    past = page_table.shape[1] * cache_a_k.shape[1]
    h = jnp.concatenate([x_prompt.reshape(n_p, d), x_sample.reshape(n_s, d)], axis=0)

    slopes = jnp.exp2(-8.0 * jnp.arange(1, A_HEADS + 1, dtype=F32) / A_HEADS)
    lam_pack = jnp.pad(jnp.stack([lam_q1, lam_k1, lam_q2, lam_k2]), ((0, 4), (0, LANES - A_DH)))
    cos_p, sin_p = _rope_tables(jnp.arange(tp, dtype=jnp.int32))
    cos_s, sin_s = _rope_tables(past + jnp.arange(ts, dtype=jnp.int32))

    proj0 = _norm_proj(h, ln0_mix, w_in0.astype(BF16))
    a_w = A_HEADS * 2 * A_DH
    a_o_p = _diff_prompt(proj0, slopes, lam_pack, a_subln, bp, tp)
    a_o_s = _diff_sample(proj0, n_p, cache_a_k, cache_a_v, page_table, slopes, lam_pack, a_subln, db, ts)
    b_o_p, b_state_p = _retention(proj0, 0, cos_p, sin_p, jnp.zeros((bp, B_HEADS, B_DK, B_DV), F32), bp, tp)
    b_o_s, b_state_s = _retention(proj0, n_p, cos_s, sin_s, state_b.astype(F32), db, ts)
    a_o = jnp.concatenate([a_o_p, a_o_s], axis=0)
    b_o = jnp.concatenate([b_o_p, b_o_s], axis=0)
    w_out0_b = w_out0.astype(BF16)
    wr0, br0 = _router_pack(moe0_wg, moe0_bg, moe0_we, moe0_be)
    h, xn, ids, wts = _out_route([a_o, b_o], [w_out0_b[:A_HEADS * A_DV], w_out0_b[A_HEADS * A_DV:]], h, ln0_ffn, wr0, br0)
    h = _moe(h, xn, ids[:, :MOE_TOPK], wts[:, :MOE_TOPK], *_expert_weights(moe0_w_gate, moe0_w_up, moe0_w_down))

    proj1 = _norm_proj(h, ln1_mix, w_in1.astype(BF16))
    c_w = C_HEADS * C_DH
    c_o_p = _stick_prompt(proj1, bp, tp)
    c_o_s = _stick_sample(proj1, n_p, cache_c_k, cache_c_v, page_table, db, ts)
    c_o = jnp.concatenate([c_o_p, c_o_s], axis=0)
    wr1, br1 = _router_pack(moe1_wg, moe1_bg, moe1_we, moe1_be)
    h, xn, ids, wts = _out_route([c_o], [w_out1.astype(BF16)], h, ln1_ffn, wr1, br1)
    h = _moe(h, xn, ids[:, :MOE_TOPK], wts[:, :MOE_TOPK], *_expert_weights(moe1_w_gate, moe1_w_up, moe1_w_down))
    y = _rmsnorm(h, ln_f)

    def split(a, lo, width, shape_p, shape_s):
        return a[:n_p, lo:lo + width].reshape(shape_p), a[n_p:, lo:lo + width].reshape(shape_s)

    y_p, y_s = y[:n_p].reshape(bp, tp, d), y[n_p:].reshape(db, ts, d)
    a_k_p, a_k_s = split(proj0, a_w, a_w, (bp, tp, A_HEADS, 2 * A_DH), (db, ts, A_HEADS, 2 * A_DH))
    a_v_p, a_v_s = split(proj0, 2 * a_w, A_HEADS * A_DV, (bp, tp, A_HEADS, A_DV), (db, ts, A_HEADS, A_DV))
    c_k_p, c_k_s = split(proj1, c_w, c_w, (bp, tp, C_HEADS, C_DH), (db, ts, C_HEADS, C_DH))
    c_v_p, c_v_s = split(proj1, 2 * c_w, c_w, (bp, tp, C_HEADS, C_DH), (db, ts, C_HEADS, C_DH))
    return (y_p, y_s, a_k_p, a_v_p, b_state_p, c_k_p, c_v_p, a_k_s, a_v_s, b_state_s, c_k_s, c_v_s)
```

```python
import functools
import math

import jax
import jax.numpy as jnp
from jax import lax
from jax.experimental import pallas as pl
from jax.experimental.pallas import tpu as pltpu

F32 = jnp.float32
BF16 = jnp.bfloat16

D_MODEL = 1024
EPS = 1e-6
A_HEADS, A_DH, A_DV = 4, 64, 128
LAMBDA_INIT = 0.8 - 0.6 * math.exp(-0.3 * 0)
B_HEADS, B_DK, B_DV = 4, 64, 128
CHUNK = 128
ROPE_BASE = 10000.0
C_HEADS, C_DH = 16, 64
MOE_GROUPS, MOE_EXPERTS, MOE_TOPK = 4, 8, 2
MOE_FF = D_MODEL // 2
N_EXPERTS = MOE_GROUPS * MOE_EXPERTS

LANES = 128
ROW_TILE = 256
ATT_TILE_A = 512
ATT_TILE_CQ, ATT_TILE_CK = 256, 128
PAGE_GROUP = 4
STICK_DEAD = -104.0
LOG2E = math.log2(math.e)
VMEM_LIMIT = 56 * 1024 * 1024


def _params(sem, vmem=VMEM_LIMIT):
    return pltpu.CompilerParams(dimension_semantics=sem, vmem_limit_bytes=vmem)


def _dot(a, b):
    return jnp.dot(a, b, preferred_element_type=F32)


def _dot_nt(a, b):
    return lax.dot_general(a, b, (((1,), (1,)), ((), ())), preferred_element_type=F32)


def _dot_tn(a, b):
    return lax.dot_general(a, b, (((0,), (0,)), ((), ())), preferred_element_type=F32)


def _rms(x):
    return x * lax.rsqrt(jnp.mean(x * x, axis=-1, keepdims=True) + EPS)


def _norm_proj_kernel(x_ref, g_ref, w_ref, o_ref):
    xn = _rms(x_ref[...]) * g_ref[...]
    o_ref[...] = _dot(xn.astype(BF16), w_ref[...])


def _norm_proj(x, g, w):
    n, d = x.shape
    nout = w.shape[1]
    return pl.pallas_call(
        _norm_proj_kernel,
        grid=(n // ROW_TILE,),
        in_specs=[
            pl.BlockSpec((ROW_TILE, d), lambda i: (i, 0)),
            pl.BlockSpec((1, d), lambda i: (0, 0)),
            pl.BlockSpec((d, nout), lambda i: (0, 0)),
        ],
        out_specs=pl.BlockSpec((ROW_TILE, nout), lambda i: (i, 0)),
        out_shape=jax.ShapeDtypeStruct((n, nout), F32),
        compiler_params=_params(("parallel",)),
        name="norm_proj",
    )(x, g.reshape(1, d), w)


def _route(logits):
    lane = lax.broadcasted_iota(jnp.int32, logits.shape, 1).astype(F32)
    neg = jnp.float32(-jnp.inf)
    big = jnp.float32(LANES)
    gl = jnp.where(lane < MOE_GROUPS, logits, neg)
    gmax = jnp.max(gl, axis=-1, keepdims=True)
    g_idx = jnp.min(jnp.where(gl == gmax, lane, big), axis=-1, keepdims=True)
    g_w = 1.0 / jnp.sum(jnp.exp(gl - gmax), axis=-1, keepdims=True)
    lo = MOE_GROUPS + g_idx * MOE_EXPERTS
    el = jnp.where((lane >= lo) & (lane < lo + MOE_EXPERTS), logits, neg)
    v1 = jnp.max(el, axis=-1, keepdims=True)
    i1 = jnp.min(jnp.where(el == v1, lane, big), axis=-1, keepdims=True)
    el2 = jnp.where(lane == i1, neg, el)
    v2 = jnp.max(el2, axis=-1, keepdims=True)
    i2 = jnp.min(jnp.where(el2 == v2, lane, big), axis=-1, keepdims=True)
    e2 = jnp.exp(v2 - v1)
    w1 = g_w / (1.0 + e2)
    w2 = g_w * e2 / (1.0 + e2)
    ids = jnp.where(lane == 0, i1 - MOE_GROUPS, jnp.where(lane == 1, i2 - MOE_GROUPS, 0.0))
    wts = jnp.where(lane == 0, w1, jnp.where(lane == 1, w2, 0.0))
    return ids.astype(jnp.int32), wts


def _out_route_kernel(*refs, n_a):
    a_refs = refs[:n_a]
    w_refs = refs[n_a:2 * n_a]
    h_ref, g_ref, wr_ref, br_ref, ho_ref, xn_ref, ids_ref, wts_ref = refs[2 * n_a:]
    h = h_ref[...]
    for a_ref, w_ref in zip(a_refs, w_refs):
        h = h + _dot(a_ref[...].astype(BF16), w_ref[...])
    ho_ref[...] = h
    xn = _rms(h) * g_ref[...]
    xn_ref[...] = xn
    logits = jnp.dot(xn, wr_ref[...], preferred_element_type=F32, precision=lax.Precision.HIGHEST) + br_ref[...]
    ids, wts = _route(logits)
    ids_ref[...] = ids
    wts_ref[...] = wts


def _out_route(a_list, w_list, h, g, wr, br):
    n, d = h.shape
    n_a = len(a_list)
    row = lambda i: (i, 0)
    const = lambda i: (0, 0)
    in_specs = [pl.BlockSpec((ROW_TILE, a.shape[1]), row) for a in a_list]
    in_specs += [pl.BlockSpec(w.shape, const) for w in w_list]
    in_specs += [pl.BlockSpec((ROW_TILE, d), row), pl.BlockSpec((1, d), const),
                 pl.BlockSpec((d, LANES), const), pl.BlockSpec((1, LANES), const)]
    return pl.pallas_call(
        functools.partial(_out_route_kernel, n_a=n_a),
        grid=(n // ROW_TILE,),
        in_specs=in_specs,
        out_specs=[pl.BlockSpec((ROW_TILE, d), row), pl.BlockSpec((ROW_TILE, d), row),
                   pl.BlockSpec((ROW_TILE, LANES), row), pl.BlockSpec((ROW_TILE, LANES), row)],
        out_shape=[jax.ShapeDtypeStruct((n, d), F32), jax.ShapeDtypeStruct((n, d), F32),
                   jax.ShapeDtypeStruct((n, LANES), jnp.int32), jax.ShapeDtypeStruct((n, LANES), F32)],
        compiler_params=_params(("parallel",)),
        name="out_route",
    )(*a_list, *w_list, h, g.reshape(1, d), wr, br)


def _moe_dispatch_kernel(pos_ref, x_ref, xs_init, xs_hbm, sem):
    del xs_init
    i = pl.program_id(0)

    def issue(r, c):
        for k in range(MOE_TOPK):
            dst = pos_ref[(i * ROW_TILE + r) * MOE_TOPK + k]
            pltpu.make_async_copy(x_ref.at[pl.ds(r, 1), :], xs_hbm.at[pl.ds(dst, 1), :], sem).start()
        return c

    lax.fori_loop(0, ROW_TILE, issue, 0)
    for k in range(MOE_TOPK):
        pltpu.make_async_copy(x_ref, xs_hbm.at[pl.ds(0, ROW_TILE), :], sem).wait()


def _moe_dispatch(xn, pos, n_rows):
    n, d = xn.shape
    grid_spec = pltpu.PrefetchScalarGridSpec(
        num_scalar_prefetch=1,
        grid=(n // ROW_TILE,),
        in_specs=[pl.BlockSpec((ROW_TILE, d), lambda i, pos: (i, 0)), pl.BlockSpec(memory_space=pl.ANY)],
        out_specs=pl.BlockSpec(memory_space=pl.ANY),
        scratch_shapes=[pltpu.SemaphoreType.DMA(())],
    )
    return pl.pallas_call(
        _moe_dispatch_kernel,
        grid_spec=grid_spec,
        out_shape=jax.ShapeDtypeStruct((n_rows, d), F32),
        input_output_aliases={2: 0},
        compiler_params=_params(("arbitrary",)),
        name="moe_dispatch",
    )(pos, xn, jnp.zeros((n_rows, d), F32))


def _moe_ffn_kernel(te_ref, nv_ref, x_ref, wg_ref, wu_ref, wd_ref, o_ref):
    t = pl.program_id(0)

    @pl.when(t < nv_ref[0])
    def _():
        x = x_ref[...].astype(BF16)
        hid = jax.nn.silu(_dot(x, wg_ref[0])) * _dot(x, wu_ref[0])
        o_ref[...] = _dot(hid.astype(BF16), wd_ref[0])

    @pl.when(t >= nv_ref[0])
    def _():
        o_ref[...] = jnp.zeros_like(o_ref)


def _moe_ffn(x_sorted, tile_expert, n_valid, w_gate, w_up, w_down):
    p, d = x_sorted.shape
    f = w_gate.shape[2]
    used = lambda t, te, nv: (jnp.maximum(jnp.minimum(t, nv[0] - 1), 0), 0)
    grid_spec = pltpu.PrefetchScalarGridSpec(
        num_scalar_prefetch=2,
        grid=(p // ROW_TILE,),
        in_specs=[
            pl.BlockSpec((ROW_TILE, d), used),
            pl.BlockSpec((1, d, f), lambda t, te, nv: (te[t], 0, 0)),
            pl.BlockSpec((1, d, f), lambda t, te, nv: (te[t], 0, 0)),
            pl.BlockSpec((1, f, d), lambda t, te, nv: (te[t], 0, 0)),
        ],
        out_specs=pl.BlockSpec((ROW_TILE, d), lambda t, te, nv: (t, 0)),
    )
    return pl.pallas_call(
        _moe_ffn_kernel,
        grid_spec=grid_spec,
        out_shape=jax.ShapeDtypeStruct((p, d), F32),
        compiler_params=_params(("arbitrary",)),
        name="moe_ffn",
    )(tile_expert, n_valid, x_sorted, w_gate, w_up, w_down)


def _moe_combine_kernel(pos_ref, h_ref, w_ref, *refs, final_norm):
    g_ref = refs[0] if final_norm else None
    y_hbm, o_ref, rbuf, sem = refs[-4:]
    i = pl.program_id(0)

    def issue(r, c):
        for k in range(MOE_TOPK):
            src = pos_ref[(i * ROW_TILE + r) * MOE_TOPK + k]
            pltpu.make_async_copy(y_hbm.at[pl.ds(src, 1), :], rbuf.at[k, pl.ds(r, 1), :], sem).start()
        return c

    lax.fori_loop(0, ROW_TILE, issue, 0)
    for k in range(MOE_TOPK):
        pltpu.make_async_copy(y_hbm.at[pl.ds(0, ROW_TILE), :], rbuf.at[k], sem).wait()
    w = w_ref[...]
    acc = h_ref[...]
    for k in range(MOE_TOPK):
        acc = acc + w[:, k:k + 1] * rbuf[k]
    if final_norm:
        acc = _rms(acc) * g_ref[...]
    o_ref[...] = acc


def _moe_combine(h, wts, y_sorted, pos, gain=None):
    n, d = h.shape
    row = lambda i, pos: (i, 0)
    in_specs = [pl.BlockSpec((ROW_TILE, d), row), pl.BlockSpec((ROW_TILE, LANES), row)]
    args = [h, wts]
    if gain is not None:
        in_specs.append(pl.BlockSpec((1, d), lambda i, pos: (0, 0)))
        args.append(gain.reshape(1, d))
    grid_spec = pltpu.PrefetchScalarGridSpec(
        num_scalar_prefetch=1,
        grid=(n // ROW_TILE,),
        in_specs=in_specs + [pl.BlockSpec(memory_space=pl.ANY)],
        out_specs=pl.BlockSpec((ROW_TILE, d), row),
        scratch_shapes=[pltpu.VMEM((MOE_TOPK, ROW_TILE, d), F32), pltpu.SemaphoreType.DMA(())],
    )
    return pl.pallas_call(
        functools.partial(_moe_combine_kernel, final_norm=gain is not None),
        grid_spec=grid_spec,
        out_shape=jax.ShapeDtypeStruct((n, d), F32),
        compiler_params=_params(("arbitrary",)),
        name="moe_combine",
    )(pos, *args, y_sorted)


def _dispatch_plan(ids):
    n = ids.shape[0]
    a = n * MOE_TOPK
    n_tiles = a // ROW_TILE + N_EXPERTS
    e_ids = jnp.arange(N_EXPERTS, dtype=jnp.int32)
    onehot = (ids.reshape(a, 1) == e_ids[None, :]).astype(jnp.int32)
    seen = jnp.cumsum(onehot, axis=0)
    counts = seen[-1]
    tiles_per = (counts + ROW_TILE - 1) // ROW_TILE
    tile_end = jnp.cumsum(tiles_per)
    tile_start = tile_end - tiles_per
    pos = jnp.sum(onehot * (seen - 1 + (tile_start * ROW_TILE)[None, :]), axis=1).astype(jnp.int32)
    n_valid = tile_end[-1].astype(jnp.int32)
    t_idx = jnp.arange(n_tiles, dtype=jnp.int32)
    te = jnp.minimum(jnp.sum((tile_end[None, :] <= t_idx[:, None]).astype(jnp.int32), axis=1), N_EXPERTS - 1)
    last = jnp.max(jnp.where(tiles_per > 0, e_ids, 0))
    tile_expert = jnp.where(t_idx < n_valid, te, last).astype(jnp.int32)
    return tile_expert, n_valid.reshape(1), pos, n_tiles * ROW_TILE


def _moe(h, xn, ids, wts, w_gate, w_up, w_down, final_gain=None):
    tile_expert, n_valid, pos, n_rows = _dispatch_plan(ids[:, :MOE_TOPK])
    x_sorted = _moe_dispatch(xn, pos, n_rows)
    y_sorted = _moe_ffn(x_sorted, tile_expert, n_valid, w_gate, w_up, w_down)
    return _moe_combine(h, wts, y_sorted, pos, final_gain)


def _lambda_from(lam_ref):
    lp = lam_ref[...]
    s1 = jnp.sum(lp[0:1, :] * lp[1:2, :], axis=-1, keepdims=True)
    s2 = jnp.sum(lp[2:3, :] * lp[3:4, :], axis=-1, keepdims=True)
    return jnp.exp(s1) - jnp.exp(s2) + LAMBDA_INIT


def _transpose_tiles(src_ref, dst_ref, tile):
    def body(j, c):
        start = pl.multiple_of(j * tile, tile)
        dst_ref[j] = src_ref[pl.ds(start, tile), :].T.astype(BF16)
        return c

    lax.fori_loop(0, dst_ref.shape[0], body, 0)


def _diff_prompt_kernel(slopes_ref, q_ref, k_ref, v_ref, lam_ref, g_ref, o_ref, vt_scr, m_scr, l_scr, acc_scr, *, tile):
    h = pl.program_id(1)
    qi = pl.program_id(2)

    @pl.when(qi == 0)
    def _():
        _transpose_tiles(v_ref, vt_scr, tile)

    slope = slopes_ref[h] * LOG2E
    q = q_ref[...] * (A_DH ** -0.5 * LOG2E)
    lane = lax.broadcasted_iota(jnp.int32, q.shape, 1)
    qcat = jnp.concatenate([jnp.where(lane < A_DH, q, 0.0), jnp.where(lane >= A_DH, q, 0.0)], axis=0).astype(BF16)
    key = lax.broadcasted_iota(jnp.int32, (tile, 2 * tile), 0)
    qry = lax.broadcasted_iota(jnp.int32, (tile, 2 * tile), 1) % tile
    bias0 = slope * key.astype(F32)
    m_scr[...] = jnp.full(m_scr.shape, -jnp.inf, F32)
    l_scr[...] = jnp.zeros(l_scr.shape, F32)
    acc_scr[...] = jnp.zeros(acc_scr.shape, F32)

    def block(j, masked):
        start = pl.multiple_of(j * tile, tile)
        k = k_ref[pl.ds(start, tile), :].astype(BF16)
        cb = slope * (j * tile).astype(F32)
        t = _dot_nt(k, qcat) + bias0
        if masked:
            t = jnp.where(key <= qry, t, -jnp.inf)
        m_old = m_scr[...]
        m_new = jnp.maximum(m_old, jnp.max(t, axis=0, keepdims=True) + cb)
        p = jnp.exp2(t + (cb - m_new))
        alpha = jnp.exp2(m_old - m_new)
        l_scr[...] = alpha * l_scr[...] + jnp.sum(p, axis=0, keepdims=True)
        acc_scr[...] = alpha * acc_scr[...] + _dot(vt_scr[j], p.astype(BF16))
        m_scr[...] = m_new

    def body(j, carry):
        block(j, False)
        return carry

    lax.fori_loop(0, qi, body, 0)
    block(qi, True)
    lam = _lambda_from(lam_ref)
    o = acc_scr[...] / l_scr[...]
    o = (o[:, :tile] - lam * o[:, tile:]).T
    o_ref[...] = _rms(o) * g_ref[...] * (1.0 - LAMBDA_INIT)


def _diff_prompt(proj, slopes, lam_pack, subln, n_batch, t_len):
    tile = min(ATT_TILE_A, t_len)
    nq = t_len // tile
    qcol, kcol, vcol = 0, A_HEADS, 2 * A_HEADS
    grid_spec = pltpu.PrefetchScalarGridSpec(
        num_scalar_prefetch=1,
        grid=(n_batch, A_HEADS, nq),
        in_specs=[
            pl.BlockSpec((tile, LANES), lambda b, h, i, s: (b * nq + i, qcol + h)),
            pl.BlockSpec((t_len, LANES), lambda b, h, i, s: (b, kcol + h)),
            pl.BlockSpec((t_len, LANES), lambda b, h, i, s: (b, vcol + h)),
            pl.BlockSpec((8, LANES), lambda b, h, i, s: (0, 0)),
            pl.BlockSpec((1, LANES), lambda b, h, i, s: (0, 0)),
        ],
        out_specs=pl.BlockSpec((tile, LANES), lambda b, h, i, s: (b * nq + i, h)),
        scratch_shapes=[pltpu.VMEM((nq, LANES, tile), BF16), pltpu.VMEM((1, 2 * tile), F32),
                        pltpu.VMEM((1, 2 * tile), F32), pltpu.VMEM((LANES, 2 * tile), F32)],
    )
    return pl.pallas_call(
        functools.partial(_diff_prompt_kernel, tile=tile),
        grid_spec=grid_spec,
        out_shape=jax.ShapeDtypeStruct((n_batch * t_len, A_HEADS * A_DV), F32),
        compiler_params=_params(("parallel", "parallel", "arbitrary")),
        name="diff_prompt",
    )(slopes, proj, proj, proj, lam_pack, subln.reshape(1, LANES))


def _diff_sample_kernel(pt_ref, slopes_ref, q_ref, kn_ref, vn_ref, *refs, ts, page, n_pages, group):
    kp_refs, vp_refs = refs[:group], refs[group:2 * group]
    lam_ref, g_ref, o_ref, m_scr, l_scr, acc_scr = refs[2 * group:]
    p = pl.program_id(1)
    hrows = 2 * ts
    rows = A_HEADS * hrows
    past = n_pages * page
    q = q_ref[...] * (A_DH ** -0.5 * LOG2E)
    lane = lax.broadcasted_iota(jnp.int32, (ts, LANES), 1)
    qh = []
    for hh in range(A_HEADS):
        qq = q[:, hh * LANES:(hh + 1) * LANES]
        qh.append(jnp.concatenate([jnp.where(lane < A_DH, qq, 0.0), jnp.where(lane >= A_DH, qq, 0.0)], axis=0).astype(BF16))
    rid = lax.broadcasted_iota(jnp.int32, (rows, 1), 0)
    rt = rid % ts
    slope_rows = jnp.zeros((rows, 1), F32)
    for hh in range(A_HEADS):
        slope_rows = jnp.where(rid // hrows == hh, slopes_ref[hh] * LOG2E, slope_rows)

    def update(k_of, v_of, kpos, new_tokens):
        s = jnp.concatenate([jnp.concatenate([_dot_nt(qh[hh], k.astype(BF16)) for k in k_of(hh)], axis=1)
                             for hh in range(A_HEADS)], axis=0)
        t = s + slope_rows * kpos.astype(F32)
        if new_tokens:
            t = jnp.where(kpos - past <= rt, t, -jnp.inf)
        m_old = m_scr[...]
        m_new = jnp.maximum(m_old, jnp.max(t, axis=-1, keepdims=True))
        pr = jnp.exp2(t - m_new)
        alpha = jnp.exp2(m_old - m_new)
        l_scr[...] = alpha * l_scr[...] + jnp.sum(pr, axis=-1, keepdims=True)
        for hh in range(A_HEADS):
            sl = slice(hh * hrows, (hh + 1) * hrows)
            pv, off = 0.0, 0
            for v in v_of(hh):
                pv = pv + _dot(pr[sl, off:off + v.shape[0]].astype(BF16), v.astype(BF16))
                off += v.shape[0]
            acc_scr[hh] = alpha[sl] * acc_scr[hh] + pv
        m_scr[...] = m_new

    @pl.when(p == 0)
    def _():
        m_scr[...] = jnp.full(m_scr.shape, -jnp.inf, F32)
        l_scr[...] = jnp.zeros(l_scr.shape, F32)
        acc_scr[...] = jnp.zeros(acc_scr.shape, F32)
        kpos = past + lax.broadcasted_iota(jnp.int32, (1, ts), 1)
        update(lambda hh: [kn_ref[:, hh * LANES:(hh + 1) * LANES]], lambda hh: [vn_ref[:, hh * A_DV:(hh + 1) * A_DV]],
               kpos, True)

    kpos = p * (group * page) + lax.broadcasted_iota(jnp.int32, (1, group * page), 1)
    update(lambda hh: [r[0, pl.ds(hh, page, stride=A_HEADS), :] for r in kp_refs],
           lambda hh: [r[0, pl.ds(hh, page, stride=A_HEADS), :] for r in vp_refs], kpos, False)

    @pl.when(p == n_pages // group - 1)
    def _():
        lam = _lambda_from(lam_ref)
        g = g_ref[...]
        inv_l = 1.0 / l_scr[...]
        outs = []
        for hh in range(A_HEADS):
            o = acc_scr[hh] * inv_l[hh * hrows:(hh + 1) * hrows]
            o = o[:ts] - lam * o[ts:]
            outs.append(_rms(o) * g * (1.0 - LAMBDA_INIT))
        o_ref[...] = jnp.concatenate(outs, axis=-1)


def _diff_sample(proj, row0, cache_k, cache_v, page_table, slopes, lam_pack, subln, n_seq, ts):
    n_pages = page_table.shape[1]
    page = cache_k.shape[1]
    width = A_HEADS * 2 * A_DH
    vwidth = A_HEADS * A_DV
    rb = row0 // ts
    rows = 2 * A_HEADS * ts
    qblk, kblk, vblk = 0, 1, 2
    group = math.gcd(n_pages, PAGE_GROUP)

    def page_spec(i, w):
        return pl.BlockSpec((1, page * A_HEADS, w), lambda s, p, pt, sl: (pt[s, p * group + i], 0, 0))

    grid_spec = pltpu.PrefetchScalarGridSpec(
        num_scalar_prefetch=2,
        grid=(n_seq, n_pages // group),
        in_specs=[
            pl.BlockSpec((ts, width), lambda s, p, pt, sl: (rb + s, qblk)),
            pl.BlockSpec((ts, width), lambda s, p, pt, sl: (rb + s, kblk)),
            pl.BlockSpec((ts, vwidth), lambda s, p, pt, sl: (rb + s, vblk)),
            *[page_spec(i, 2 * A_DH) for i in range(group)],
            *[page_spec(i, A_DV) for i in range(group)],
            pl.BlockSpec((8, LANES), lambda s, p, pt, sl: (0, 0)),
            pl.BlockSpec((1, LANES), lambda s, p, pt, sl: (0, 0)),
        ],
        out_specs=pl.BlockSpec((ts, vwidth), lambda s, p, pt, sl: (s, 0)),
        scratch_shapes=[pltpu.VMEM((rows, 1), F32), pltpu.VMEM((rows, 1), F32),
                        pltpu.VMEM((A_HEADS, 2 * ts, A_DV), F32)],
    )
    k_pages = cache_k.reshape(-1, page * A_HEADS, 2 * A_DH)
    v_pages = cache_v.reshape(-1, page * A_HEADS, A_DV)
    return pl.pallas_call(
        functools.partial(_diff_sample_kernel, ts=ts, page=page, n_pages=n_pages, group=group),
        grid_spec=grid_spec,
        out_shape=jax.ShapeDtypeStruct((n_seq * ts, vwidth), F32),
        compiler_params=_params(("parallel", "arbitrary")),
        name="diff_sample",
    )(page_table, slopes, proj, proj, proj, *([k_pages] * group), *([v_pages] * group), lam_pack,
      subln.reshape(1, LANES))


def _retention_kernel(q_ref, k_ref, v_ref, g_ref, cos_ref, sin_ref, s0_ref, o_ref, s_out_ref, s_scr, *, chunk):
    c = pl.program_id(1)

    @pl.when(c == 0)
    def _():
        s_scr[...] = s0_ref[0]

    cos = cos_ref[...]
    sin = sin_ref[...]
    width = B_HEADS * B_DK
    lane = lax.broadcasted_iota(jnp.int32, (chunk, width), 1)
    first_half = (lane % B_DK) < (B_DK // 2)

    def rope(x):
        partner = jnp.where(first_half, pltpu.roll(x, width - B_DK // 2, 1), pltpu.roll(x, B_DK // 2, 1))
        return x * cos + partner * sin

    q = rope(q_ref[...])
    k = rope(k_ref[...]) * (B_DK ** -0.5)
    v = v_ref[...]
    g = g_ref[...]
    n_col = lax.broadcasted_iota(jnp.int32, (chunk, 1), 0).astype(F32)
    ri = lax.broadcasted_iota(jnp.int32, (chunk, chunk), 0)
    ci = lax.broadcasted_iota(jnp.int32, (chunk, chunk), 1)
    dist = (ri - ci).astype(F32)
    outs = []
    for hh in range(B_HEADS):
        log_g = math.log1p(-(2.0 ** (-5.0 - hh)))
        decay = jnp.where(ri >= ci, jnp.exp(log_g * jnp.maximum(dist, 0.0)), 0.0)
        qh = q[:, hh * B_DK:(hh + 1) * B_DK]
        kh = k[:, hh * B_DK:(hh + 1) * B_DK]
        vh = v[:, hh * B_DV:(hh + 1) * B_DV].astype(BF16)
        s_h = s_scr[hh]
        inner = _dot_nt(qh.astype(BF16), kh.astype(BF16)) * decay
        q_dec = qh * jnp.exp(log_g * (n_col + 1.0))
        o = _dot(inner.astype(BF16), vh) + _dot(q_dec.astype(BF16), s_h.astype(BF16))
        k_dec = kh * jnp.exp(log_g * (chunk - 1.0 - n_col))
        s_scr[hh] = math.exp(log_g * chunk) * s_h + _dot_tn(k_dec.astype(BF16), vh)
        gh = g[:, hh * B_DV:(hh + 1) * B_DV]
        outs.append(jax.nn.silu(gh) * _rms(o))
    o_ref[...] = jnp.concatenate(outs, axis=-1)

    @pl.when(c == pl.num_programs(1) - 1)
    def _():
        s_out_ref[0] = s_scr[...]


def _retention(proj, row0, cos_t, sin_t, s0, n_seq, t_len):
    chunk = CHUNK if (t_len > CHUNK and t_len % CHUNK == 0) else t_len
    nc = t_len // chunk
    rb = row0 // chunk
    qw, vw = B_HEADS * B_DK, B_HEADS * B_DV
    qcol, kcol, vcol, gcol = 6, 7, 4, 5
    row = lambda s, c: (rb + s * nc + c, 0)
    return pl.pallas_call(
        functools.partial(_retention_kernel, chunk=chunk),
        grid=(n_seq, nc),
        in_specs=[
            pl.BlockSpec((chunk, qw), lambda s, c: (rb + s * nc + c, qcol)),
            pl.BlockSpec((chunk, qw), lambda s, c: (rb + s * nc + c, kcol)),
            pl.BlockSpec((chunk, vw), lambda s, c: (rb + s * nc + c, vcol)),
            pl.BlockSpec((chunk, vw), lambda s, c: (rb + s * nc + c, gcol)),
            pl.BlockSpec((chunk, qw), lambda s, c: (c, 0)),
            pl.BlockSpec((chunk, qw), lambda s, c: (c, 0)),
            pl.BlockSpec((1, B_HEADS, B_DK, B_DV), lambda s, c: (s, 0, 0, 0)),
        ],
        out_specs=[pl.BlockSpec((chunk, vw), lambda s, c: (s * nc + c, 0)),
                   pl.BlockSpec((1, B_HEADS, B_DK, B_DV), lambda s, c: (s, 0, 0, 0))],
        out_shape=[jax.ShapeDtypeStruct((n_seq * t_len, vw), F32),
                   jax.ShapeDtypeStruct((n_seq, B_HEADS, B_DK, B_DV), F32)],
        scratch_shapes=[pltpu.VMEM((B_HEADS, B_DK, B_DV), F32)],
        compiler_params=_params(("parallel", "arbitrary")),
        name="retention",
    )(proj, proj, proj, proj, cos_t, sin_t, s0)


def _rope_tables(pos):
    half = B_DK // 2
    inv = ROPE_BASE ** (-jnp.arange(half, dtype=F32) / half)
    ang = pos.astype(F32)[:, None] * inv[None, :]
    cos, sin = jnp.cos(ang), jnp.sin(ang)
    cos_t = jnp.tile(jnp.concatenate([cos, cos], axis=-1), (1, B_HEADS))
    sin_t = jnp.tile(jnp.concatenate([-sin, sin], axis=-1), (1, B_HEADS))
    return cos_t, sin_t


def _softplus(z):
    return jnp.maximum(z, 0.0) + jnp.log1p(jnp.exp(-jnp.abs(z)))


def _split_bf16(x):
    hi = x.astype(BF16)
    return hi, (x - hi.astype(F32)).astype(BF16)


def _stick_weights_t(zt, carry, newer_mat, mask):
    sp = _softplus(zt)
    lk = -sp
    if mask is not None:
        lk = jnp.where(mask, lk, 0.0)
    hi, lo = _split_bf16(lk)
    newer = _dot(newer_mat, hi) + _dot(newer_mat, lo)
    a = jnp.exp((zt - sp) + newer + carry)
    if mask is not None:
        a = jnp.where(mask, a, 0.0)
    return a, carry + jnp.sum(lk, axis=0, keepdims=True)


def _stick_prompt_kernel(q_ref, k_ref, v_ref, o_ref, vt_scr, carry_scr, acc_scr, *, tq, tk):
    qi = pl.program_id(2)

    @pl.when(qi == 0)
    def _():
        _transpose_tiles(v_ref, vt_scr, tk)

    heads = LANES // C_DH
    width = heads * tq
    q = q_ref[...] * (C_DH ** -0.5)
    lane = lax.broadcasted_iota(jnp.int32, q.shape, 1)
    qcat = jnp.concatenate([jnp.where(lane // C_DH == hh, q, 0.0) for hh in range(heads)], axis=0).astype(BF16)
    key = lax.broadcasted_iota(jnp.int32, (tk, width), 0)
    qry = lax.broadcasted_iota(jnp.int32, (tk, width), 1) % tq
    rj = lax.broadcasted_iota(jnp.int32, (tk, tk), 0)
    cs = lax.broadcasted_iota(jnp.int32, (tk, tk), 1)
    newer_mat = jnp.where(cs > rj, 1.0, 0.0).astype(BF16)
    sub = tq // tk

    def block(j, mask, carry):
        start = pl.multiple_of(j * tk, tk)
        zt = _dot_nt(k_ref[pl.ds(start, tk), :].astype(BF16), qcat)
        a, carry = _stick_weights_t(zt, carry, newer_mat, mask)
        return _dot(vt_scr[j], a.astype(BF16)), carry

    carry = jnp.zeros((1, width), F32)
    for jj in reversed(range(sub)):
        av, carry = block(qi * sub + jj, (jj * tk + key) < qry, carry)
        if jj == sub - 1:
            acc_scr[...] = av
        else:
            acc_scr[...] += av
    carry_scr[...] = carry

    def cond(state):
        j, cmax = state
        return (j >= 0) & (cmax > STICK_DEAD)

    def body(state):
        j, _ = state
        av, carry = block(j, None, carry_scr[...])
        av2, carry2 = block(jnp.maximum(j - 1, 0), None, carry)
        acc_scr[...] += jnp.where(j >= 1, av + av2, av)
        carry = jnp.where(j >= 1, carry2, carry)
        carry_scr[...] = carry
        return j - 2, jnp.max(carry)

    lax.while_loop(cond, body, (qi * sub - 1, jnp.max(carry)))
    acc = acc_scr[...]
    o_t = jnp.concatenate([acc[hh * C_DH:(hh + 1) * C_DH, hh * tq:(hh + 1) * tq] for hh in range(heads)], axis=0)
    o_ref[...] = o_t.T


def _stick_prompt(proj, n_batch, t_len):
    tq = min(ATT_TILE_CQ, t_len)
    tk = min(ATT_TILE_CK, tq)
    nq = t_len // tq
    ng = C_HEADS * C_DH // LANES
    return pl.pallas_call(
        functools.partial(_stick_prompt_kernel, tq=tq, tk=tk),
        grid=(n_batch, ng, nq),
        in_specs=[
            pl.BlockSpec((tq, LANES), lambda b, g, i: (b * nq + i, g)),
            pl.BlockSpec((t_len, LANES), lambda b, g, i: (b, ng + g)),
            pl.BlockSpec((t_len, LANES), lambda b, g, i: (b, 2 * ng + g)),
        ],
        out_specs=pl.BlockSpec((tq, LANES), lambda b, g, i: (b * nq + i, g)),
        out_shape=jax.ShapeDtypeStruct((n_batch * t_len, C_HEADS * C_DH), F32),
        scratch_shapes=[pltpu.VMEM((t_len // tk, LANES, tk), BF16), pltpu.VMEM((1, LANES // C_DH * tq), F32),
                        pltpu.VMEM((LANES, LANES // C_DH * tq), F32)],
        compiler_params=_params(("parallel", "parallel", "arbitrary")),
        name="stick_prompt",
    )(proj, proj, proj)


def _stick_sample_kernel(pt_ref, q_ref, kn_ref, vn_ref, ck_hbm, cv_hbm, o_ref, kbuf, vbuf, ksem, vsem,
                         carry_scr, acc_scr, *, ts, page, n_pages):
    s = pl.program_id(0)
    rows = C_HEADS * ts

    def copies(seq, i, slot):
        pid = pt_ref[seq, n_pages - 1 - i]
        return (pltpu.make_async_copy(ck_hbm.at[pid], kbuf.at[slot], ksem.at[slot]),
                pltpu.make_async_copy(cv_hbm.at[pid], vbuf.at[slot], vsem.at[slot]))

    def start(seq, i, slot):
        for cp in copies(seq, i, slot):
            cp.start()

    def wait(seq, i, slot):
        for cp in copies(seq, i, slot):
            cp.wait()

    @pl.when(s == 0)
    def _():
        start(0, 0, 0)
        start(0, 1, 1)

    q = q_ref[...] * (C_DH ** -0.5)
    qh = [q[:, hh * C_DH:(hh + 1) * C_DH].astype(BF16) for hh in range(C_HEADS)]
    r = lax.broadcasted_iota(jnp.int32, (page, page), 0)
    c = lax.broadcasted_iota(jnp.int32, (page, page), 1)
    upper = jnp.where(r > c, 1.0, 0.0).astype(BF16)

    def tile(z_of, av_of, carry, mask):
        z = jnp.concatenate([z_of(hh) for hh in range(C_HEADS)], axis=0)
        sp = _softplus(z)
        lk = -sp
        if mask is not None:
            lk = jnp.where(mask, lk, 0.0)
        hi, lo = _split_bf16(lk)
        newer = _dot(hi, upper) + _dot(lo, upper)
        a = jnp.exp((z - sp) + newer + carry)
        if mask is not None:
            a = jnp.where(mask, a, 0.0)
        for hh in range(C_HEADS):
            acc_scr[hh] += av_of(hh, a[hh * ts:(hh + 1) * ts].astype(BF16))
        return carry + jnp.sum(lk, axis=-1, keepdims=True)

    acc_scr[...] = jnp.zeros(acc_scr.shape, F32)
    pad = jnp.zeros((page - ts, C_DH), F32)
    rt = lax.broadcasted_iota(jnp.int32, (rows, page), 0) % ts
    kc = lax.broadcasted_iota(jnp.int32, (rows, page), 1)

    def new_rows(ref, hh):
        return jnp.concatenate([ref[:, hh * C_DH:(hh + 1) * C_DH], pad], axis=0).astype(BF16)

    carry = tile(lambda hh: _dot_nt(qh[hh], new_rows(kn_ref, hh)), lambda hh, a: _dot(a, new_rows(vn_ref, hh)),
                 jnp.zeros((rows, 1), F32), kc < rt)

    def page_tile(slot, carry):
        return tile(lambda hh: _dot(qh[hh], kbuf[slot, hh].astype(BF16)),
                    lambda hh, a: _dot_nt(a, vbuf[slot, hh].astype(BF16)), carry, None)

    wait(s, 0, 0)
    carry = page_tile(0, carry)
    cmax = jnp.max(carry)
    carry_scr[...] = carry

    @pl.when((2 < n_pages) & (cmax > STICK_DEAD))
    def _():
        start(s, 2, 0)

    def cond(state):
        i, cmax = state
        return (i < n_pages) & (cmax > STICK_DEAD)

    def body(state):
        i, _ = state
        slot = i % 2
        wait(s, i, slot)
        carry = page_tile(slot, carry_scr[...])
        cmax = jnp.max(carry)
        carry_scr[...] = carry

        @pl.when((i + 2 < n_pages) & (cmax > STICK_DEAD))
        def _():
            start(s, i + 2, slot)

        return i + 1, cmax

    i_end, _ = lax.while_loop(cond, body, (jnp.int32(1), cmax))

    @pl.when(i_end < n_pages)
    def _():
        wait(s, i_end, i_end % 2)

    @pl.when(s + 1 < pl.num_programs(0))
    def _():
        start(s + 1, 0, 0)
        start(s + 1, 1, 1)

    o_ref[0] = acc_scr[...]


def _stick_sample(proj, row0, cache_k, cache_v, page_table, n_seq, ts):
    n_pages = page_table.shape[1]
    page = cache_k.shape[1]
    assert n_pages >= 2
    width = C_HEADS * C_DH
    rb = row0 // ts
    grid_spec = pltpu.PrefetchScalarGridSpec(
        num_scalar_prefetch=1,
        grid=(n_seq,),
        in_specs=[
            pl.BlockSpec((ts, width), lambda s, pt: (rb + s, 0)),
            pl.BlockSpec((ts, width), lambda s, pt: (rb + s, 1)),
            pl.BlockSpec((ts, width), lambda s, pt: (rb + s, 2)),
            pl.BlockSpec(memory_space=pl.ANY),
            pl.BlockSpec(memory_space=pl.ANY),
        ],
        out_specs=pl.BlockSpec((1, C_HEADS, ts, C_DH), lambda s, pt: (s, 0, 0, 0)),
        scratch_shapes=[pltpu.VMEM((2, C_HEADS, C_DH, page), F32), pltpu.VMEM((2, C_HEADS, C_DH, page), F32),
                        pltpu.SemaphoreType.DMA((2,)), pltpu.SemaphoreType.DMA((2,)),
                        pltpu.VMEM((C_HEADS * ts, 1), F32), pltpu.VMEM((C_HEADS, ts, C_DH), F32)],
    )
    out = pl.pallas_call(
        functools.partial(_stick_sample_kernel, ts=ts, page=page, n_pages=n_pages),
        grid_spec=grid_spec,
        out_shape=jax.ShapeDtypeStruct((n_seq, C_HEADS, ts, C_DH), F32),
        compiler_params=_params(("arbitrary",)),
        name="stick_sample",
    )(page_table, proj, proj, proj, jnp.transpose(cache_k, (0, 2, 3, 1)), jnp.transpose(cache_v, (0, 2, 3, 1)))
    return jnp.transpose(out, (0, 2, 1, 3)).reshape(n_seq * ts, width)


def _router_pack(wg, bg, we, be):
    d = wg.shape[0]
    we_flat = jnp.transpose(we, (1, 0, 2)).reshape(d, N_EXPERTS)
    w = jnp.concatenate([wg, we_flat], axis=1)
    b = jnp.concatenate([bg, be.reshape(N_EXPERTS)])
    pad = LANES - w.shape[1]
    return jnp.pad(w, ((0, 0), (0, pad))), jnp.pad(b, (0, pad)).reshape(1, LANES)


def _expert_weights(w_gate, w_up, w_down):
    d, f = w_gate.shape[2], w_gate.shape[3]
    return (w_gate.reshape(N_EXPERTS, d, f).astype(BF16), w_up.reshape(N_EXPERTS, d, f).astype(BF16),
            w_down.reshape(N_EXPERTS, f, d).astype(BF16))


def kernel(x_prompt, x_sample, cache_a_k, cache_a_v, state_b, cache_c_k, cache_c_v, page_table, ln0_mix, w_in0, lam_q1, lam_k1, lam_q2, lam_k2, a_subln, w_out0, ln0_ffn, moe0_wg, moe0_bg, moe0_we, moe0_be, moe0_w_gate, moe0_w_up, moe0_w_down, ln1_mix, w_in1, w_out1, ln1_ffn, moe1_wg, moe1_bg, moe1_we, moe1_be, moe1_w_gate, moe1_w_up, moe1_w_down, ln_f):
    bp, tp, d = x_prompt.shape
    db, ts, _ = x_sample.shape
    n_p, n_s = bp * tp, db * ts
    past = page_table.shape[1] * cache_a_k.shape[1]
    h = jnp.concatenate([x_prompt.reshape(n_p, d), x_sample.reshape(n_s, d)], axis=0)

    slopes = jnp.exp2(-8.0 * jnp.arange(1, A_HEADS + 1, dtype=F32) / A_HEADS)
    lam_pack = jnp.pad(jnp.stack([lam_q1, lam_k1, lam_q2, lam_k2]), ((0, 4), (0, LANES - A_DH)))
    cos_p, sin_p = _rope_tables(jnp.arange(tp, dtype=jnp.int32))
    cos_s, sin_s = _rope_tables(past + jnp.arange(ts, dtype=jnp.int32))

    proj0 = _norm_proj(h, ln0_mix, w_in0.astype(BF16))
    a_w = A_HEADS * 2 * A_DH
    a_o_p = _diff_prompt(proj0, slopes, lam_pack, a_subln, bp, tp)
    a_o_s = _diff_sample(proj0, n_p, cache_a_k, cache_a_v, page_table, slopes, lam_pack, a_subln, db, ts)
    b_o_p, b_state_p = _retention(proj0, 0, cos_p, sin_p, jnp.zeros((bp, B_HEADS, B_DK, B_DV), F32), bp, tp)
    b_o_s, b_state_s = _retention(proj0, n_p, cos_s, sin_s, state_b.astype(F32), db, ts)
    a_o = jnp.concatenate([a_o_p, a_o_s], axis=0)
    b_o = jnp.concatenate([b_o_p, b_o_s], axis=0)
    w_out0_b = w_out0.astype(BF16)
    wr0, br0 = _router_pack(moe0_wg, moe0_bg, moe0_we, moe0_be)
    h, xn, ids, wts = _out_route([a_o, b_o], [w_out0_b[:A_HEADS * A_DV], w_out0_b[A_HEADS * A_DV:]], h, ln0_ffn, wr0, br0)
    h = _moe(h, xn, ids, wts, *_expert_weights(moe0_w_gate, moe0_w_up, moe0_w_down))

    proj1 = _norm_proj(h, ln1_mix, w_in1.astype(BF16))
    c_w = C_HEADS * C_DH
    c_o_p = _stick_prompt(proj1, bp, tp)
    c_o_s = _stick_sample(proj1, n_p, cache_c_k, cache_c_v, page_table, db, ts)
    c_o = jnp.concatenate([c_o_p, c_o_s], axis=0)
    wr1, br1 = _router_pack(moe1_wg, moe1_bg, moe1_we, moe1_be)
    h, xn, ids, wts = _out_route([c_o], [w_out1.astype(BF16)], h, ln1_ffn, wr1, br1)
    y = _moe(h, xn, ids, wts, *_expert_weights(moe1_w_gate, moe1_w_up, moe1_w_down), final_gain=ln_f)

    def split(a, lo, width, shape_p, shape_s):
        return a[:n_p, lo:lo + width].reshape(shape_p), a[n_p:, lo:lo + width].reshape(shape_s)

    y_p, y_s = y[:n_p].reshape(bp, tp, d), y[n_p:].reshape(db, ts, d)
    a_k_p, a_k_s = split(proj0, a_w, a_w, (bp, tp, A_HEADS, 2 * A_DH), (db, ts, A_HEADS, 2 * A_DH))
    a_v_p, a_v_s = split(proj0, 2 * a_w, A_HEADS * A_DV, (bp, tp, A_HEADS, A_DV), (db, ts, A_HEADS, A_DV))
    c_k_p, c_k_s = split(proj1, c_w, c_w, (bp, tp, C_HEADS, C_DH), (db, ts, C_HEADS, C_DH))
    c_v_p, c_v_s = split(proj1, 2 * c_w, c_w, (bp, tp, C_HEADS, C_DH), (db, ts, C_HEADS, C_DH))
    return (y_p, y_s, a_k_p, a_v_p, b_state_p, c_k_p, c_v_p, a_k_s, a_v_s, b_state_s, c_k_s, c_v_s)
```

```python
import functools
import math

import jax
import jax.numpy as jnp
from jax import lax
from jax.experimental import pallas as pl
from jax.experimental.pallas import tpu as pltpu

F32 = jnp.float32
BF16 = jnp.bfloat16

D_MODEL = 1024
EPS = 1e-6
A_HEADS, A_DH, A_DV = 4, 64, 128
LAMBDA_INIT = 0.8 - 0.6 * math.exp(-0.3 * 0)
B_HEADS, B_DK, B_DV = 4, 64, 128
CHUNK = 128
ROPE_BASE = 10000.0
C_HEADS, C_DH = 16, 64
MOE_GROUPS, MOE_EXPERTS, MOE_TOPK = 4, 8, 2
MOE_FF = D_MODEL // 2
N_EXPERTS = MOE_GROUPS * MOE_EXPERTS

LANES = 128
ROW_TILE = 256
DMA_UNROLL = 8
ATT_TILE_A = 512
ATT_TILE_CQ, ATT_TILE_CK = 512, 256
PAGE_GROUP = 8
LOG2E = math.log2(math.e)
STICK_DEAD = -104.0 * LOG2E
VMEM_LIMIT = 56 * 1024 * 1024


def _params(sem, vmem=VMEM_LIMIT):
    return pltpu.CompilerParams(dimension_semantics=sem, vmem_limit_bytes=vmem)


def _dot(a, b):
    return jnp.dot(a, b, preferred_element_type=F32)


def _dot_nt(a, b):
    return lax.dot_general(a, b, (((1,), (1,)), ((), ())), preferred_element_type=F32)


def _dot_tn(a, b):
    return lax.dot_general(a, b, (((0,), (0,)), ((), ())), preferred_element_type=F32)


def _rms(x):
    return x * lax.rsqrt(jnp.mean(x * x, axis=-1, keepdims=True) + EPS)


def _norm_proj_kernel(x_ref, g_ref, w_ref, o_ref, *head_refs, head_cols):
    xn = _rms(x_ref[...]) * g_ref[...]
    o = _dot(xn.astype(BF16), w_ref[...])
    o_ref[...] = o
    for ref, (col, heads) in zip(head_refs, head_cols):
        for hh in range(heads):
            ref[pl.ds(hh, ROW_TILE, stride=heads), :] = o[:, col + hh * LANES:col + (hh + 1) * LANES]


def _norm_proj(x, g, w, head_cols=()):
    n, d = x.shape
    nout = w.shape[1]
    out_specs = [pl.BlockSpec((ROW_TILE, nout), lambda i: (i, 0))]
    out_shape = [jax.ShapeDtypeStruct((n, nout), F32)]
    for _, heads in head_cols:
        out_specs.append(pl.BlockSpec((ROW_TILE * heads, LANES), lambda i: (i, 0)))
        out_shape.append(jax.ShapeDtypeStruct((n * heads, LANES), F32))
    return pl.pallas_call(
        functools.partial(_norm_proj_kernel, head_cols=tuple(head_cols)),
        grid=(n // ROW_TILE,),
        in_specs=[
            pl.BlockSpec((ROW_TILE, d), lambda i: (i, 0)),
            pl.BlockSpec((1, d), lambda i: (0, 0)),
            pl.BlockSpec((d, nout), lambda i: (0, 0)),
        ],
        out_specs=out_specs,
        out_shape=out_shape,
        compiler_params=_params(("parallel",)),
        name="norm_proj",
    )(x, g.reshape(1, d), w)


def _route(logits):
    lane = lax.broadcasted_iota(jnp.int32, logits.shape, 1).astype(F32)
    neg = jnp.float32(-jnp.inf)
    big = jnp.float32(LANES)
    gl = jnp.where(lane < MOE_GROUPS, logits, neg)
    gmax = jnp.max(gl, axis=-1, keepdims=True)
    g_idx = jnp.min(jnp.where(gl == gmax, lane, big), axis=-1, keepdims=True)
    g_w = 1.0 / jnp.sum(jnp.exp(gl - gmax), axis=-1, keepdims=True)
    lo = MOE_GROUPS + g_idx * MOE_EXPERTS
    el = jnp.where((lane >= lo) & (lane < lo + MOE_EXPERTS), logits, neg)
    v1 = jnp.max(el, axis=-1, keepdims=True)
    i1 = jnp.min(jnp.where(el == v1, lane, big), axis=-1, keepdims=True)
    el2 = jnp.where(lane == i1, neg, el)
    v2 = jnp.max(el2, axis=-1, keepdims=True)
    i2 = jnp.min(jnp.where(el2 == v2, lane, big), axis=-1, keepdims=True)
    e2 = jnp.exp(v2 - v1)
    w1 = g_w / (1.0 + e2)
    w2 = g_w * e2 / (1.0 + e2)
    ids = jnp.where(lane == 0, i1 - MOE_GROUPS, jnp.where(lane == 1, i2 - MOE_GROUPS, 0.0))
    wts = jnp.where(lane == 0, w1, jnp.where(lane == 1, w2, 0.0))
    return ids.astype(jnp.int32), wts


def _out_route_kernel(*refs, n_a):
    a_refs = refs[:n_a]
    w_refs = refs[n_a:2 * n_a]
    h_ref, g_ref, wr_ref, br_ref, ho_ref, xn_ref, ids_ref, wts_ref = refs[2 * n_a:]
    h = h_ref[...]
    for a_ref, w_ref in zip(a_refs, w_refs):
        h = h + _dot(a_ref[...].astype(BF16), w_ref[...])
    ho_ref[...] = h
    xn = _rms(h) * g_ref[...]
    xn_ref[...] = xn
    logits = jnp.dot(xn, wr_ref[...], preferred_element_type=F32, precision=lax.Precision.HIGHEST) + br_ref[...]
    ids, wts = _route(logits)
    ids_ref[...] = ids
    wts_ref[...] = wts


def _out_route(a_list, w_list, h, g, wr, br):
    n, d = h.shape
    n_a = len(a_list)
    row = lambda i: (i, 0)
    const = lambda i: (0, 0)
    in_specs = [pl.BlockSpec((ROW_TILE, a.shape[1]), row) for a in a_list]
    in_specs += [pl.BlockSpec(w.shape, const) for w in w_list]
    in_specs += [pl.BlockSpec((ROW_TILE, d), row), pl.BlockSpec((1, d), const),
                 pl.BlockSpec((d, LANES), const), pl.BlockSpec((1, LANES), const)]
    return pl.pallas_call(
        functools.partial(_out_route_kernel, n_a=n_a),
        grid=(n // ROW_TILE,),
        in_specs=in_specs,
        out_specs=[pl.BlockSpec((ROW_TILE, d), row), pl.BlockSpec((ROW_TILE, d), row),
                   pl.BlockSpec((ROW_TILE, LANES), row), pl.BlockSpec((ROW_TILE, LANES), row)],
        out_shape=[jax.ShapeDtypeStruct((n, d), F32), jax.ShapeDtypeStruct((n, d), F32),
                   jax.ShapeDtypeStruct((n, LANES), jnp.int32), jax.ShapeDtypeStruct((n, LANES), F32)],
        compiler_params=_params(("parallel",)),
        name="out_route",
    )(*a_list, *w_list, h, g.reshape(1, d), wr, br)


def _moe_dispatch_kernel(pos_ref, last_ref, nv_ref, x_ref, xs_hbm, zero_scr, sem):
    i = pl.program_id(0)
    n_tiles = xs_hbm.shape[0] // ROW_TILE

    def zero_tile(t):
        start = pl.multiple_of(t * ROW_TILE, ROW_TILE)
        return pltpu.make_async_copy(zero_scr, xs_hbm.at[pl.ds(start, ROW_TILE), :], sem)

    @pl.when(i == 0)
    def _():
        zero_scr[...] = jnp.zeros(zero_scr.shape, F32)
        for e in range(N_EXPERTS):
            @pl.when(last_ref[e] >= 0)
            def _():
                zero_tile(last_ref[e]).start()

        def fill(t, c):
            zero_tile(t).start()
            return c

        lax.fori_loop(nv_ref[0], n_tiles, fill, 0)
        for e in range(N_EXPERTS):
            @pl.when(last_ref[e] >= 0)
            def _():
                zero_tile(0).wait()

        def drain(t, c):
            zero_tile(0).wait()
            return c

        lax.fori_loop(nv_ref[0], n_tiles, drain, 0)

    def issue(r, c):
        for k in range(MOE_TOPK):
            dst = pos_ref[(i * ROW_TILE + r) * MOE_TOPK + k]
            pltpu.make_async_copy(x_ref.at[pl.ds(r, 1), :], xs_hbm.at[pl.ds(dst, 1), :], sem).start()
        return c

    lax.fori_loop(0, ROW_TILE, issue, 0, unroll=DMA_UNROLL)
    for k in range(MOE_TOPK):
        pltpu.make_async_copy(x_ref, xs_hbm.at[pl.ds(0, ROW_TILE), :], sem).wait()


def _moe_dispatch(xn, pos, last_tile, n_valid, n_rows):
    n, d = xn.shape
    grid_spec = pltpu.PrefetchScalarGridSpec(
        num_scalar_prefetch=3,
        grid=(n // ROW_TILE,),
        in_specs=[pl.BlockSpec((ROW_TILE, d), lambda i, pos, last, nv: (i, 0))],
        out_specs=pl.BlockSpec(memory_space=pl.ANY),
        scratch_shapes=[pltpu.VMEM((ROW_TILE, d), F32), pltpu.SemaphoreType.DMA(())],
    )
    return pl.pallas_call(
        _moe_dispatch_kernel,
        grid_spec=grid_spec,
        out_shape=jax.ShapeDtypeStruct((n_rows, d), F32),
        compiler_params=_params(("arbitrary",)),
        name="moe_dispatch",
    )(pos, last_tile, n_valid, xn)


def _moe_ffn_kernel(te_ref, nv_ref, x_ref, wg_ref, wu_ref, wd_ref, o_ref):
    t = pl.program_id(0)

    @pl.when(t < nv_ref[0])
    def _():
        x = x_ref[...].astype(BF16)
        hid = jax.nn.silu(_dot(x, wg_ref[0].astype(BF16))) * _dot(x, wu_ref[0].astype(BF16))
        o_ref[...] = _dot(hid.astype(BF16), wd_ref[0].astype(BF16))

    @pl.when(t >= nv_ref[0])
    def _():
        o_ref[...] = jnp.zeros_like(o_ref)


def _moe_ffn(x_sorted, tile_expert, n_valid, w_gate, w_up, w_down):
    p, d = x_sorted.shape
    f = w_gate.shape[2]
    used = lambda t, te, nv: (jnp.maximum(jnp.minimum(t, nv[0] - 1), 0), 0)
    grid_spec = pltpu.PrefetchScalarGridSpec(
        num_scalar_prefetch=2,
        grid=(p // ROW_TILE,),
        in_specs=[
            pl.BlockSpec((ROW_TILE, d), used),
            pl.BlockSpec((1, d, f), lambda t, te, nv: (te[t], 0, 0)),
            pl.BlockSpec((1, d, f), lambda t, te, nv: (te[t], 0, 0)),
            pl.BlockSpec((1, f, d), lambda t, te, nv: (te[t], 0, 0)),
        ],
        out_specs=pl.BlockSpec((ROW_TILE, d), lambda t, te, nv: (t, 0)),
    )
    return pl.pallas_call(
        _moe_ffn_kernel,
        grid_spec=grid_spec,
        out_shape=jax.ShapeDtypeStruct((p, d), F32),
        compiler_params=_params(("arbitrary",)),
        name="moe_ffn",
    )(tile_expert, n_valid, x_sorted, w_gate, w_up, w_down)


def _moe_combine_kernel(pos_ref, h_ref, w_ref, *refs, final_norm):
    g_ref = refs[0] if final_norm else None
    y_hbm, o_ref, rbuf, sem = refs[-4:]
    i = pl.program_id(0)

    def issue(r, c):
        for k in range(MOE_TOPK):
            src = pos_ref[(i * ROW_TILE + r) * MOE_TOPK + k]
            pltpu.make_async_copy(y_hbm.at[pl.ds(src, 1), :], rbuf.at[k, pl.ds(r, 1), :], sem).start()
        return c

    lax.fori_loop(0, ROW_TILE, issue, 0, unroll=DMA_UNROLL)
    for k in range(MOE_TOPK):
        pltpu.make_async_copy(y_hbm.at[pl.ds(0, ROW_TILE), :], rbuf.at[k], sem).wait()
    w = w_ref[...]
    acc = h_ref[...]
    for k in range(MOE_TOPK):
        acc = acc + w[:, k:k + 1] * rbuf[k]
    if final_norm:
        acc = _rms(acc) * g_ref[...]
    o_ref[...] = acc


def _moe_combine(h, wts, y_sorted, pos, gain=None):
    n, d = h.shape
    row = lambda i, pos: (i, 0)
    in_specs = [pl.BlockSpec((ROW_TILE, d), row), pl.BlockSpec((ROW_TILE, LANES), row)]
    args = [h, wts]
    if gain is not None:
        in_specs.append(pl.BlockSpec((1, d), lambda i, pos: (0, 0)))
        args.append(gain.reshape(1, d))
    grid_spec = pltpu.PrefetchScalarGridSpec(
        num_scalar_prefetch=1,
        grid=(n // ROW_TILE,),
        in_specs=in_specs + [pl.BlockSpec(memory_space=pl.ANY)],
        out_specs=pl.BlockSpec((ROW_TILE, d), row),
        scratch_shapes=[pltpu.VMEM((MOE_TOPK, ROW_TILE, d), F32), pltpu.SemaphoreType.DMA(())],
    )
    return pl.pallas_call(
        functools.partial(_moe_combine_kernel, final_norm=gain is not None),
        grid_spec=grid_spec,
        out_shape=jax.ShapeDtypeStruct((n, d), F32),
        compiler_params=_params(("arbitrary",)),
        name="moe_combine",
    )(pos, *args, y_sorted)


def _dispatch_plan(ids):
    n = ids.shape[0]
    a = n * MOE_TOPK
    n_tiles = a // ROW_TILE + N_EXPERTS
    e_ids = jnp.arange(N_EXPERTS, dtype=jnp.int32)
    onehot = (ids.reshape(a, 1) == e_ids[None, :]).astype(jnp.int32)
    seen = jnp.cumsum(onehot, axis=0)
    counts = seen[-1]
    tiles_per = (counts + ROW_TILE - 1) // ROW_TILE
    tile_end = jnp.cumsum(tiles_per)
    tile_start = tile_end - tiles_per
    pos = jnp.sum(onehot * (seen - 1 + (tile_start * ROW_TILE)[None, :]), axis=1).astype(jnp.int32)
    n_valid = tile_end[-1].astype(jnp.int32)
    t_idx = jnp.arange(n_tiles, dtype=jnp.int32)
    te = jnp.minimum(jnp.sum((tile_end[None, :] <= t_idx[:, None]).astype(jnp.int32), axis=1), N_EXPERTS - 1)
    last = jnp.max(jnp.where(tiles_per > 0, e_ids, 0))
    tile_expert = jnp.where(t_idx < n_valid, te, last).astype(jnp.int32)
    last_tile = jnp.where(tiles_per > 0, tile_end - 1, -1).astype(jnp.int32)
    return tile_expert, n_valid.reshape(1), pos, last_tile, n_tiles * ROW_TILE


def _moe(h, xn, ids, wts, w_gate, w_up, w_down, final_gain=None):
    tile_expert, n_valid, pos, last_tile, n_rows = _dispatch_plan(ids[:, :MOE_TOPK])
    x_sorted = _moe_dispatch(xn, pos, last_tile, n_valid, n_rows)
    y_sorted = _moe_ffn(x_sorted, tile_expert, n_valid, w_gate, w_up, w_down)
    return _moe_combine(h, wts, y_sorted, pos, final_gain)


def _lambda_from(lam_ref):
    lp = lam_ref[...]
    s1 = jnp.sum(lp[0:1, :] * lp[1:2, :], axis=-1, keepdims=True)
    s2 = jnp.sum(lp[2:3, :] * lp[3:4, :], axis=-1, keepdims=True)
    return jnp.exp(s1) - jnp.exp(s2) + LAMBDA_INIT


def _transpose_tiles(src_ref, dst_ref, tile):
    def body(j, c):
        start = pl.multiple_of(j * tile, tile)
        dst_ref[j] = src_ref[pl.ds(start, tile), :].T.astype(BF16)
        return c

    lax.fori_loop(0, dst_ref.shape[0], body, 0)


def _diff_prompt_kernel(slopes_ref, q_ref, k_ref, v_ref, lam_ref, g_ref, o_ref, vt_scr, s_scr, m_scr, l_scr, acc_scr,
                        *, tq, tk):
    h = pl.program_id(1)
    qi = pl.program_id(2)

    @pl.when(qi == 0)
    def _():
        _transpose_tiles(v_ref, vt_scr, tk)

    slope = slopes_ref[h] * LOG2E
    q = q_ref[...] * (A_DH ** -0.5 * LOG2E)
    lane = lax.broadcasted_iota(jnp.int32, q.shape, 1)
    qcat = jnp.concatenate([jnp.where(lane < A_DH, q, 0.0), jnp.where(lane >= A_DH, q, 0.0)], axis=0).astype(BF16)
    key = lax.broadcasted_iota(jnp.int32, (tk, 2 * tq), 0)
    qry = lax.broadcasted_iota(jnp.int32, (tk, 2 * tq), 1) % tq
    bias0 = slope * key.astype(F32)
    m_scr[...] = jnp.full(m_scr.shape, -jnp.inf, F32)
    l_scr[...] = jnp.zeros(l_scr.shape, F32)
    acc_scr[...] = jnp.zeros(acc_scr.shape, F32)

    def scores(j):
        start = pl.multiple_of(j * tk, tk)
        return _dot_nt(k_ref[pl.ds(start, tk), :].astype(BF16), qcat)

    def update(slot, j, mask):
        cb = slope * (j * tk).astype(F32)
        t = s_scr[slot] + bias0
        if mask is not None:
            t = jnp.where(mask, t, -jnp.inf)
        m_old = m_scr[...]
        m_new = jnp.maximum(m_old, jnp.max(t, axis=0, keepdims=True) + cb)
        p = jnp.exp2(t + (cb - m_new))
        alpha = jnp.exp2(m_old - m_new)
        l_scr[...] = alpha * l_scr[...] + jnp.sum(p, axis=0, keepdims=True)
        acc_scr[...] = alpha * acc_scr[...] + _dot(vt_scr[j], p.astype(BF16))
        m_scr[...] = m_new

    s_scr[0] = scores(0)

    def body(i, carry):
        j = 2 * i
        s_scr[1] = scores(j + 1)
        update(0, j, None)
        s_scr[0] = scores(j + 2)
        update(1, j + 1, None)
        return carry

    lax.fori_loop(0, qi // 2, body, 0)
    diag = key <= qry

    @pl.when(qi % 2 == 1)
    def _():
        s_scr[1] = scores(qi)
        update(0, qi - 1, None)
        update(1, qi, diag)

    @pl.when(qi % 2 == 0)
    def _():
        update(0, qi, diag)

    lam = _lambda_from(lam_ref)
    o = acc_scr[...] / l_scr[...]
    o = (o[:, :tq] - lam * o[:, tq:]).T
    o_ref[...] = _rms(o) * g_ref[...] * (1.0 - LAMBDA_INIT)


def _diff_prompt(proj, slopes, lam_pack, subln, n_batch, t_len):
    tq = tk = min(ATT_TILE_A, t_len)
    nq = t_len // tq
    qcol, kcol, vcol = 0, A_HEADS, 2 * A_HEADS
    grid_spec = pltpu.PrefetchScalarGridSpec(
        num_scalar_prefetch=1,
        grid=(n_batch, A_HEADS, nq),
        in_specs=[
            pl.BlockSpec((tq, LANES), lambda b, h, i, s: (b * nq + i, qcol + h)),
            pl.BlockSpec((t_len, LANES), lambda b, h, i, s: (b, kcol + h)),
            pl.BlockSpec((t_len, LANES), lambda b, h, i, s: (b, vcol + h)),
            pl.BlockSpec((8, LANES), lambda b, h, i, s: (0, 0)),
            pl.BlockSpec((1, LANES), lambda b, h, i, s: (0, 0)),
        ],
        out_specs=pl.BlockSpec((tq, LANES), lambda b, h, i, s: (b * nq + i, h)),
        scratch_shapes=[pltpu.VMEM((t_len // tk, LANES, tk), BF16), pltpu.VMEM((2, tk, 2 * tq), F32),
                        pltpu.VMEM((1, 2 * tq), F32), pltpu.VMEM((1, 2 * tq), F32), pltpu.VMEM((LANES, 2 * tq), F32)],
    )
    return pl.pallas_call(
        functools.partial(_diff_prompt_kernel, tq=tq, tk=tk),
        grid_spec=grid_spec,
        out_shape=jax.ShapeDtypeStruct((n_batch * t_len, A_HEADS * A_DV), F32),
        compiler_params=_params(("parallel", "parallel", "arbitrary")),
        name="diff_prompt",
    )(slopes, proj, proj, proj, lam_pack, subln.reshape(1, LANES))


def _diff_sample_kernel(pt_ref, slopes_ref, q_ref, kn_ref, vn_ref, *refs, ts, page, n_pages, group):
    kp_refs, vp_refs = refs[:group], refs[group:2 * group]
    lam_ref, g_ref, o_ref, m_scr, l_scr, acc_scr = refs[2 * group:]
    p = pl.program_id(1)
    hrows = 2 * ts
    rows = A_HEADS * hrows
    past = n_pages * page
    q = q_ref[...] * (A_DH ** -0.5 * LOG2E)
    lane = lax.broadcasted_iota(jnp.int32, (ts, LANES), 1)
    qh = []
    for hh in range(A_HEADS):
        qq = q[:, hh * LANES:(hh + 1) * LANES]
        qh.append(jnp.concatenate([jnp.where(lane < A_DH, qq, 0.0), jnp.where(lane >= A_DH, qq, 0.0)], axis=0).astype(BF16))
    rid = lax.broadcasted_iota(jnp.int32, (rows, 1), 0)
    rt = rid % ts
    slope_rows = jnp.zeros((rows, 1), F32)
    for hh in range(A_HEADS):
        slope_rows = jnp.where(rid // hrows == hh, slopes_ref[hh] * LOG2E, slope_rows)

    def update(k_of, v_of, kpos, new_tokens):
        s = jnp.concatenate([jnp.concatenate([_dot_nt(qh[hh], k.astype(BF16)) for k in k_of(hh)], axis=1)
                             for hh in range(A_HEADS)], axis=0)
        t = s + slope_rows * kpos.astype(F32)
        if new_tokens:
            t = jnp.where(kpos - past <= rt, t, -jnp.inf)
        m_old = m_scr[...]
        m_new = jnp.maximum(m_old, jnp.max(t, axis=-1, keepdims=True))
        pr = jnp.exp2(t - m_new)
        alpha = jnp.exp2(m_old - m_new)
        l_scr[...] = alpha * l_scr[...] + jnp.sum(pr, axis=-1, keepdims=True)
        for hh in range(A_HEADS):
            sl = slice(hh * hrows, (hh + 1) * hrows)
            pv, off = 0.0, 0
            for v in v_of(hh):
                pv = pv + _dot(pr[sl, off:off + v.shape[0]].astype(BF16), v.astype(BF16))
                off += v.shape[0]
            acc_scr[hh] = alpha[sl] * acc_scr[hh] + pv
        m_scr[...] = m_new

    @pl.when(p == 0)
    def _():
        m_scr[...] = jnp.full(m_scr.shape, -jnp.inf, F32)
        l_scr[...] = jnp.zeros(l_scr.shape, F32)
        acc_scr[...] = jnp.zeros(acc_scr.shape, F32)
        kpos = past + lax.broadcasted_iota(jnp.int32, (1, ts), 1)
        update(lambda hh: [kn_ref[:, hh * LANES:(hh + 1) * LANES]], lambda hh: [vn_ref[:, hh * A_DV:(hh + 1) * A_DV]],
               kpos, True)

    kpos = p * (group * page) + lax.broadcasted_iota(jnp.int32, (1, group * page), 1)
    update(lambda hh: [r[0, pl.ds(hh, page, stride=A_HEADS), :] for r in kp_refs],
           lambda hh: [r[0, pl.ds(hh, page, stride=A_HEADS), :] for r in vp_refs], kpos, False)

    @pl.when(p == n_pages // group - 1)
    def _():
        lam = _lambda_from(lam_ref)
        g = g_ref[...]
        inv_l = 1.0 / l_scr[...]
        outs = []
        for hh in range(A_HEADS):
            o = acc_scr[hh] * inv_l[hh * hrows:(hh + 1) * hrows]
            o = o[:ts] - lam * o[ts:]
            outs.append(_rms(o) * g * (1.0 - LAMBDA_INIT))
        o_ref[...] = jnp.concatenate(outs, axis=-1)


def _diff_sample(proj, row0, cache_k, cache_v, page_table, slopes, lam_pack, subln, n_seq, ts):
    n_pages = page_table.shape[1]
    page = cache_k.shape[1]
    width = A_HEADS * 2 * A_DH
    vwidth = A_HEADS * A_DV
    rb = row0 // ts
    rows = 2 * A_HEADS * ts
    qblk, kblk, vblk = 0, 1, 2
    group = math.gcd(n_pages, PAGE_GROUP)

    def page_spec(i, w):
        return pl.BlockSpec((1, page * A_HEADS, w), lambda s, p, pt, sl: (pt[s, p * group + i], 0, 0))

    grid_spec = pltpu.PrefetchScalarGridSpec(
        num_scalar_prefetch=2,
        grid=(n_seq, n_pages // group),
        in_specs=[
            pl.BlockSpec((ts, width), lambda s, p, pt, sl: (rb + s, qblk)),
            pl.BlockSpec((ts, width), lambda s, p, pt, sl: (rb + s, kblk)),
            pl.BlockSpec((ts, vwidth), lambda s, p, pt, sl: (rb + s, vblk)),
            *[page_spec(i, 2 * A_DH) for i in range(group)],
            *[page_spec(i, A_DV) for i in range(group)],
            pl.BlockSpec((8, LANES), lambda s, p, pt, sl: (0, 0)),
            pl.BlockSpec((1, LANES), lambda s, p, pt, sl: (0, 0)),
        ],
        out_specs=pl.BlockSpec((ts, vwidth), lambda s, p, pt, sl: (s, 0)),
        scratch_shapes=[pltpu.VMEM((rows, 1), F32), pltpu.VMEM((rows, 1), F32),
                        pltpu.VMEM((A_HEADS, 2 * ts, A_DV), F32)],
    )
    k_pages = cache_k.reshape(-1, page * A_HEADS, 2 * A_DH)
    v_pages = cache_v.reshape(-1, page * A_HEADS, A_DV)
    return pl.pallas_call(
        functools.partial(_diff_sample_kernel, ts=ts, page=page, n_pages=n_pages, group=group),
        grid_spec=grid_spec,
        out_shape=jax.ShapeDtypeStruct((n_seq * ts, vwidth), F32),
        compiler_params=_params(("parallel", "arbitrary")),
        name="diff_sample",
    )(page_table, slopes, proj, proj, proj, *([k_pages] * group), *([v_pages] * group), lam_pack,
      subln.reshape(1, LANES))


def _retention_kernel(q_ref, k_ref, v_ref, g_ref, cos_ref, sin_ref, s0_ref, o_ref, s_out_ref, s_scr, *, chunk):
    c = pl.program_id(1)

    @pl.when(c == 0)
    def _():
        s_scr[...] = s0_ref[0]

    cos = cos_ref[...]
    sin = sin_ref[...]
    width = B_HEADS * B_DK
    lane = lax.broadcasted_iota(jnp.int32, (chunk, width), 1)
    first_half = (lane % B_DK) < (B_DK // 2)

    def rope(x):
        partner = jnp.where(first_half, pltpu.roll(x, width - B_DK // 2, 1), pltpu.roll(x, B_DK // 2, 1))
        return x * cos + partner * sin

    q = rope(q_ref[...])
    k = rope(k_ref[...]) * (B_DK ** -0.5)
    v = v_ref[...]
    g = g_ref[...]
    n_col = lax.broadcasted_iota(jnp.int32, (chunk, 1), 0).astype(F32)
    ri = lax.broadcasted_iota(jnp.int32, (chunk, chunk), 0)
    ci = lax.broadcasted_iota(jnp.int32, (chunk, chunk), 1)
    dist = (ri - ci).astype(F32)
    outs = []
    for hh in range(B_HEADS):
        log_g = math.log1p(-(2.0 ** (-5.0 - hh)))
        decay = jnp.where(ri >= ci, jnp.exp(log_g * jnp.maximum(dist, 0.0)), 0.0)
        qh = q[:, hh * B_DK:(hh + 1) * B_DK]
        kh = k[:, hh * B_DK:(hh + 1) * B_DK]
        vh = v[:, hh * B_DV:(hh + 1) * B_DV].astype(BF16)
        s_h = s_scr[hh]
        inner = _dot_nt(qh.astype(BF16), kh.astype(BF16)) * decay
        q_dec = qh * jnp.exp(log_g * (n_col + 1.0))
        o = _dot(inner.astype(BF16), vh) + _dot(q_dec.astype(BF16), s_h.astype(BF16))
        k_dec = kh * jnp.exp(log_g * (chunk - 1.0 - n_col))
        s_scr[hh] = math.exp(log_g * chunk) * s_h + _dot_tn(k_dec.astype(BF16), vh)
        gh = g[:, hh * B_DV:(hh + 1) * B_DV]
        outs.append(jax.nn.silu(gh) * _rms(o))
    o_ref[...] = jnp.concatenate(outs, axis=-1)

    @pl.when(c == pl.num_programs(1) - 1)
    def _():
        s_out_ref[0] = s_scr[...]


def _retention(proj, row0, cos_t, sin_t, s0, n_seq, t_len):
    chunk = CHUNK if (t_len > CHUNK and t_len % CHUNK == 0) else t_len
    nc = t_len // chunk
    rb = row0 // chunk
    qw, vw = B_HEADS * B_DK, B_HEADS * B_DV
    qcol, kcol, vcol, gcol = 6, 7, 4, 5
    row = lambda s, c: (rb + s * nc + c, 0)
    return pl.pallas_call(
        functools.partial(_retention_kernel, chunk=chunk),
        grid=(n_seq, nc),
        in_specs=[
            pl.BlockSpec((chunk, qw), lambda s, c: (rb + s * nc + c, qcol)),
            pl.BlockSpec((chunk, qw), lambda s, c: (rb + s * nc + c, kcol)),
            pl.BlockSpec((chunk, vw), lambda s, c: (rb + s * nc + c, vcol)),
            pl.BlockSpec((chunk, vw), lambda s, c: (rb + s * nc + c, gcol)),
            pl.BlockSpec((chunk, qw), lambda s, c: (c, 0)),
            pl.BlockSpec((chunk, qw), lambda s, c: (c, 0)),
            pl.BlockSpec((1, B_HEADS, B_DK, B_DV), lambda s, c: (s, 0, 0, 0)),
        ],
        out_specs=[pl.BlockSpec((chunk, vw), lambda s, c: (s * nc + c, 0)),
                   pl.BlockSpec((1, B_HEADS, B_DK, B_DV), lambda s, c: (s, 0, 0, 0))],
        out_shape=[jax.ShapeDtypeStruct((n_seq * t_len, vw), F32),
                   jax.ShapeDtypeStruct((n_seq, B_HEADS, B_DK, B_DV), F32)],
        scratch_shapes=[pltpu.VMEM((B_HEADS, B_DK, B_DV), F32)],
        compiler_params=_params(("parallel", "arbitrary")),
        name="retention",
    )(proj, proj, proj, proj, cos_t, sin_t, s0)


def _rope_tables(pos):
    half = B_DK // 2
    inv = ROPE_BASE ** (-jnp.arange(half, dtype=F32) / half)
    ang = pos.astype(F32)[:, None] * inv[None, :]
    cos, sin = jnp.cos(ang), jnp.sin(ang)
    cos_t = jnp.tile(jnp.concatenate([cos, cos], axis=-1), (1, B_HEADS))
    sin_t = jnp.tile(jnp.concatenate([-sin, sin], axis=-1), (1, B_HEADS))
    return cos_t, sin_t


def _softplus2(z2):
    return jnp.maximum(z2, 0.0) + jnp.log2(1.0 + jnp.exp2(-jnp.abs(z2)))


def _split_bf16(x):
    hi = x.astype(BF16)
    return hi, (x - hi.astype(F32)).astype(BF16)


def _stick_weights_t(zt, carry, newer_mat, mask):
    sp = _softplus2(zt)
    lk = -sp
    if mask is not None:
        lk = jnp.where(mask, lk, 0.0)
    hi, lo = _split_bf16(lk)
    newer = _dot(newer_mat, hi) + _dot(newer_mat, lo)
    a = jnp.exp2((zt - sp) + newer + carry)
    if mask is not None:
        a = jnp.where(mask, a, 0.0)
    return a, carry + jnp.sum(lk, axis=0, keepdims=True)


def _stick_prompt_kernel(q_ref, k_ref, v_ref, o_ref, vt_scr, carry_scr, acc_scr, *, tq, tk):
    qi = pl.program_id(2)

    @pl.when(qi == 0)
    def _():
        _transpose_tiles(v_ref, vt_scr, tk)

    heads = LANES // C_DH
    width = heads * tq
    q = q_ref[...] * (C_DH ** -0.5 * LOG2E)
    lane = lax.broadcasted_iota(jnp.int32, q.shape, 1)
    qcat = jnp.concatenate([jnp.where(lane // C_DH == hh, q, 0.0) for hh in range(heads)], axis=0).astype(BF16)
    key = lax.broadcasted_iota(jnp.int32, (tk, width), 0)
    qry = lax.broadcasted_iota(jnp.int32, (tk, width), 1) % tq
    rj = lax.broadcasted_iota(jnp.int32, (tk, tk), 0)
    cs = lax.broadcasted_iota(jnp.int32, (tk, tk), 1)
    newer_mat = jnp.where(cs > rj, 1.0, 0.0).astype(BF16)
    sub = tq // tk

    def block(j, mask, carry):
        start = pl.multiple_of(j * tk, tk)
        zt = _dot_nt(k_ref[pl.ds(start, tk), :].astype(BF16), qcat)
        a, carry = _stick_weights_t(zt, carry, newer_mat, mask)
        return _dot(vt_scr[j], a.astype(BF16)), carry

    carry = jnp.zeros((1, width), F32)
    for jj in reversed(range(sub)):
        av, carry = block(qi * sub + jj, (jj * tk + key) < qry, carry)
        if jj == sub - 1:
            acc_scr[...] = av
        else:
            acc_scr[...] += av
    carry_scr[...] = carry

    def cond(state):
        j, cmax = state
        return (j >= 0) & (cmax > STICK_DEAD)

    def body(state):
        j, _ = state
        av, carry = block(j, None, carry_scr[...])
        acc_scr[...] += av
        carry_scr[...] = carry
        return j - 1, jnp.max(carry)

    lax.while_loop(cond, body, (qi * sub - 1, jnp.max(carry)))
    acc = acc_scr[...]
    o_t = jnp.concatenate([acc[hh * C_DH:(hh + 1) * C_DH, hh * tq:(hh + 1) * tq] for hh in range(heads)], axis=0)
    o_ref[...] = o_t.T


def _stick_prompt(proj, n_batch, t_len):
    tq = min(ATT_TILE_CQ, t_len)
    tk = min(ATT_TILE_CK, tq)
    nq = t_len // tq
    ng = C_HEADS * C_DH // LANES
    return pl.pallas_call(
        functools.partial(_stick_prompt_kernel, tq=tq, tk=tk),
        grid=(n_batch, ng, nq),
        in_specs=[
            pl.BlockSpec((tq, LANES), lambda b, g, i: (b * nq + i, g)),
            pl.BlockSpec((t_len, LANES), lambda b, g, i: (b, ng + g)),
            pl.BlockSpec((t_len, LANES), lambda b, g, i: (b, 2 * ng + g)),
        ],
        out_specs=pl.BlockSpec((tq, LANES), lambda b, g, i: (b * nq + i, g)),
        out_shape=jax.ShapeDtypeStruct((n_batch * t_len, C_HEADS * C_DH), F32),
        scratch_shapes=[pltpu.VMEM((t_len // tk, LANES, tk), BF16), pltpu.VMEM((1, LANES // C_DH * tq), F32),
                        pltpu.VMEM((LANES, LANES // C_DH * tq), F32)],
        compiler_params=_params(("parallel", "parallel", "arbitrary")),
        name="stick_prompt",
    )(proj, proj, proj)


def _stick_sample_kernel(pt_ref, q_ref, kn_ref, vn_ref, ck_hbm, cv_hbm, o_ref, kbuf, vbuf, ksem, vsem,
                         carry_scr, acc_scr, *, ts, page, n_pages):
    s = pl.program_id(0)
    rows = C_HEADS * ts

    def copies(seq, i, slot):
        pid = pt_ref[seq, n_pages - 1 - i]
        return (pltpu.make_async_copy(ck_hbm.at[pid], kbuf.at[slot], ksem.at[slot]),
                pltpu.make_async_copy(cv_hbm.at[pid], vbuf.at[slot], vsem.at[slot]))

    def start(seq, i, slot):
        for cp in copies(seq, i, slot):
            cp.start()

    def wait(seq, i, slot):
        for cp in copies(seq, i, slot):
            cp.wait()

    @pl.when(s == 0)
    def _():
        start(0, 0, 0)
        start(0, 1, 1)

    q = q_ref[...] * (C_DH ** -0.5 * LOG2E)
    qh = [q[:, hh * C_DH:(hh + 1) * C_DH].astype(BF16) for hh in range(C_HEADS)]
    r = lax.broadcasted_iota(jnp.int32, (page, page), 0)
    c = lax.broadcasted_iota(jnp.int32, (page, page), 1)
    upper = jnp.where(r > c, 1.0, 0.0).astype(BF16)

    def tile(z_of, av_of, carry, mask):
        z = jnp.concatenate([z_of(hh) for hh in range(C_HEADS)], axis=0)
        sp = _softplus2(z)
        lk = -sp
        if mask is not None:
            lk = jnp.where(mask, lk, 0.0)
        hi, lo = _split_bf16(lk)
        newer = _dot(hi, upper) + _dot(lo, upper)
        a = jnp.exp2((z - sp) + newer + carry)
        if mask is not None:
            a = jnp.where(mask, a, 0.0)
        for hh in range(C_HEADS):
            acc_scr[hh] += av_of(hh, a[hh * ts:(hh + 1) * ts].astype(BF16))
        return carry + jnp.sum(lk, axis=-1, keepdims=True)

    acc_scr[...] = jnp.zeros(acc_scr.shape, F32)
    pad = jnp.zeros((page - ts, C_DH), F32)
    rt = lax.broadcasted_iota(jnp.int32, (rows, page), 0) % ts
    kc = lax.broadcasted_iota(jnp.int32, (rows, page), 1)

    def new_rows(ref, hh):
        return jnp.concatenate([ref[:, hh * C_DH:(hh + 1) * C_DH], pad], axis=0).astype(BF16)

    carry = tile(lambda hh: _dot_nt(qh[hh], new_rows(kn_ref, hh)), lambda hh, a: _dot(a, new_rows(vn_ref, hh)),
                 jnp.zeros((rows, 1), F32), kc < rt)

    def page_tile(slot, carry):
        return tile(lambda hh: _dot(qh[hh], kbuf[slot, hh].astype(BF16)),
                    lambda hh, a: _dot_nt(a, vbuf[slot, hh].astype(BF16)), carry, None)

    wait(s, 0, 0)
    carry = page_tile(0, carry)
    cmax = jnp.max(carry)
    carry_scr[...] = carry

    @pl.when((2 < n_pages) & (cmax > STICK_DEAD))
    def _():
        start(s, 2, 0)

    def cond(state):
        i, cmax = state
        return (i < n_pages) & (cmax > STICK_DEAD)

    def body(state):
        i, _ = state
        slot = i % 2
        wait(s, i, slot)
        carry = page_tile(slot, carry_scr[...])
        cmax = jnp.max(carry)
        carry_scr[...] = carry

        @pl.when((i + 2 < n_pages) & (cmax > STICK_DEAD))
        def _():
            start(s, i + 2, slot)

        return i + 1, cmax

    i_end, _ = lax.while_loop(cond, body, (jnp.int32(1), cmax))

    @pl.when(i_end < n_pages)
    def _():
        wait(s, i_end, i_end % 2)

    @pl.when(s + 1 < pl.num_programs(0))
    def _():
        start(s + 1, 0, 0)
        start(s + 1, 1, 1)

    o_ref[0] = acc_scr[...]


def _stick_sample(proj, row0, cache_k, cache_v, page_table, n_seq, ts):
    n_pages = page_table.shape[1]
    page = cache_k.shape[1]
    assert n_pages >= 2
    width = C_HEADS * C_DH
    rb = row0 // ts
    grid_spec = pltpu.PrefetchScalarGridSpec(
        num_scalar_prefetch=1,
        grid=(n_seq,),
        in_specs=[
            pl.BlockSpec((ts, width), lambda s, pt: (rb + s, 0)),
            pl.BlockSpec((ts, width), lambda s, pt: (rb + s, 1)),
            pl.BlockSpec((ts, width), lambda s, pt: (rb + s, 2)),
            pl.BlockSpec(memory_space=pl.ANY),
            pl.BlockSpec(memory_space=pl.ANY),
        ],
        out_specs=pl.BlockSpec((1, C_HEADS, ts, C_DH), lambda s, pt: (s, 0, 0, 0)),
        scratch_shapes=[pltpu.VMEM((2, C_HEADS, C_DH, page), F32), pltpu.VMEM((2, C_HEADS, C_DH, page), F32),
                        pltpu.SemaphoreType.DMA((2,)), pltpu.SemaphoreType.DMA((2,)),
                        pltpu.VMEM((C_HEADS * ts, 1), F32), pltpu.VMEM((C_HEADS, ts, C_DH), F32)],
    )
    out = pl.pallas_call(
        functools.partial(_stick_sample_kernel, ts=ts, page=page, n_pages=n_pages),
        grid_spec=grid_spec,
        out_shape=jax.ShapeDtypeStruct((n_seq, C_HEADS, ts, C_DH), F32),
        compiler_params=_params(("arbitrary",)),
        name="stick_sample",
    )(page_table, proj, proj, proj, jnp.transpose(cache_k, (0, 2, 3, 1)), jnp.transpose(cache_v, (0, 2, 3, 1)))
    return jnp.transpose(out, (0, 2, 1, 3)).reshape(n_seq * ts, width)


def _router_pack(wg, bg, we, be):
    d = wg.shape[0]
    we_flat = jnp.transpose(we, (1, 0, 2)).reshape(d, N_EXPERTS)
    w = jnp.concatenate([wg, we_flat], axis=1)
    b = jnp.concatenate([bg, be.reshape(N_EXPERTS)])
    pad = LANES - w.shape[1]
    return jnp.pad(w, ((0, 0), (0, pad))), jnp.pad(b, (0, pad)).reshape(1, LANES)


def _expert_weights(w_gate, w_up, w_down):
    d, f = w_gate.shape[2], w_gate.shape[3]
    return w_gate.reshape(N_EXPERTS, d, f), w_up.reshape(N_EXPERTS, d, f), w_down.reshape(N_EXPERTS, f, d)


def kernel(x_prompt, x_sample, cache_a_k, cache_a_v, state_b, cache_c_k, cache_c_v, page_table, ln0_mix, w_in0, lam_q1, lam_k1, lam_q2, lam_k2, a_subln, w_out0, ln0_ffn, moe0_wg, moe0_bg, moe0_we, moe0_be, moe0_w_gate, moe0_w_up, moe0_w_down, ln1_mix, w_in1, w_out1, ln1_ffn, moe1_wg, moe1_bg, moe1_we, moe1_be, moe1_w_gate, moe1_w_up, moe1_w_down, ln_f):
    bp, tp, d = x_prompt.shape
    db, ts, _ = x_sample.shape
    n_p, n_s = bp * tp, db * ts
    past = page_table.shape[1] * cache_a_k.shape[1]
    h = jnp.concatenate([x_prompt.reshape(n_p, d), x_sample.reshape(n_s, d)], axis=0)

    slopes = jnp.exp2(-8.0 * jnp.arange(1, A_HEADS + 1, dtype=F32) / A_HEADS)
    lam_pack = jnp.pad(jnp.stack([lam_q1, lam_k1, lam_q2, lam_k2]), ((0, 4), (0, LANES - A_DH)))
    cos_p, sin_p = _rope_tables(jnp.arange(tp, dtype=jnp.int32))
    cos_s, sin_s = _rope_tables(past + jnp.arange(ts, dtype=jnp.int32))

    a_w = A_HEADS * 2 * A_DH
    proj0, a_k_rows, a_v_rows = _norm_proj(h, ln0_mix, w_in0.astype(BF16), ((a_w, A_HEADS), (2 * a_w, A_HEADS)))
    a_o_p = _diff_prompt(proj0, slopes, lam_pack, a_subln, bp, tp)
    a_o_s = _diff_sample(proj0, n_p, cache_a_k, cache_a_v, page_table, slopes, lam_pack, a_subln, db, ts)
    b_o_p, b_state_p = _retention(proj0, 0, cos_p, sin_p, jnp.zeros((bp, B_HEADS, B_DK, B_DV), F32), bp, tp)
    b_o_s, b_state_s = _retention(proj0, n_p, cos_s, sin_s, state_b.astype(F32), db, ts)
    a_o = jnp.concatenate([a_o_p, a_o_s], axis=0)
    b_o = jnp.concatenate([b_o_p, b_o_s], axis=0)
    w_out0_b = w_out0.astype(BF16)
    wr0, br0 = _router_pack(moe0_wg, moe0_bg, moe0_we, moe0_be)
    h, xn, ids, wts = _out_route([a_o, b_o], [w_out0_b[:A_HEADS * A_DV], w_out0_b[A_HEADS * A_DV:]], h, ln0_ffn, wr0, br0)
    h = _moe(h, xn, ids, wts, *_expert_weights(moe0_w_gate, moe0_w_up, moe0_w_down))

    (proj1,) = _norm_proj(h, ln1_mix, w_in1.astype(BF16))
    c_w = C_HEADS * C_DH
    c_o_p = _stick_prompt(proj1, bp, tp)
    c_o_s = _stick_sample(proj1, n_p, cache_c_k, cache_c_v, page_table, db, ts)
    c_o = jnp.concatenate([c_o_p, c_o_s], axis=0)
    wr1, br1 = _router_pack(moe1_wg, moe1_bg, moe1_we, moe1_be)
    h, xn, ids, wts = _out_route([c_o], [w_out1.astype(BF16)], h, ln1_ffn, wr1, br1)
    y = _moe(h, xn, ids, wts, *_expert_weights(moe1_w_gate, moe1_w_up, moe1_w_down), final_gain=ln_f)

    def split(a, lo, width, shape_p, shape_s):
        return a[:n_p, lo:lo + width].reshape(shape_p), a[n_p:, lo:lo + width].reshape(shape_s)

    y_p, y_s = y[:n_p].reshape(bp, tp, d), y[n_p:].reshape(db, ts, d)
    a_k_p, a_k_s = a_k_rows[:n_p * A_HEADS].reshape(bp, tp, A_HEADS, 2 * A_DH), a_k_rows[n_p * A_HEADS:].reshape(db, ts, A_HEADS, 2 * A_DH)
    a_v_p, a_v_s = a_v_rows[:n_p * A_HEADS].reshape(bp, tp, A_HEADS, A_DV), a_v_rows[n_p * A_HEADS:].reshape(db, ts, A_HEADS, A_DV)
    c_k_p, c_k_s = split(proj1, c_w, c_w, (bp, tp, C_HEADS, C_DH), (db, ts, C_HEADS, C_DH))
    c_v_p, c_v_s = split(proj1, 2 * c_w, c_w, (bp, tp, C_HEADS, C_DH), (db, ts, C_HEADS, C_DH))
    return (y_p, y_s, a_k_p, a_v_p, b_state_p, c_k_p, c_v_p, a_k_s, a_v_s, b_state_s, c_k_s, c_v_s)
```

```python
import functools
import math

import jax
import jax.numpy as jnp
from jax import lax
from jax.experimental import pallas as pl
from jax.experimental.pallas import tpu as pltpu

F32 = jnp.float32
BF16 = jnp.bfloat16

D_MODEL = 1024
EPS = 1e-6
A_HEADS, A_DH, A_DV = 4, 64, 128
LAMBDA_INIT = 0.8 - 0.6 * math.exp(-0.3 * 0)
B_HEADS, B_DK, B_DV = 4, 64, 128
CHUNK = 128
ROPE_BASE = 10000.0
C_HEADS, C_DH = 16, 64
MOE_GROUPS, MOE_EXPERTS, MOE_TOPK = 4, 8, 2
MOE_FF = D_MODEL // 2
N_EXPERTS = MOE_GROUPS * MOE_EXPERTS

LANES = 128
ROW_TILE = 256
DMA_UNROLL = 8
ATT_TILE_A = 512
ATT_TILE_CQ, ATT_TILE_CK = 512, 256
PAGE_GROUP = 8
LOG2E = math.log2(math.e)
STICK_DEAD = -104.0 * LOG2E
VMEM_LIMIT = 56 * 1024 * 1024


def _params(sem, vmem=VMEM_LIMIT):
    return pltpu.CompilerParams(dimension_semantics=sem, vmem_limit_bytes=vmem)


def _dot(a, b):
    return jnp.dot(a, b, preferred_element_type=F32)


def _dot_nt(a, b):
    return lax.dot_general(a, b, (((1,), (1,)), ((), ())), preferred_element_type=F32)


def _dot_tn(a, b):
    return lax.dot_general(a, b, (((0,), (0,)), ((), ())), preferred_element_type=F32)


def _rms(x):
    return x * lax.rsqrt(jnp.mean(x * x, axis=-1, keepdims=True) + EPS)


def _norm_proj_kernel(x_ref, g_ref, w_ref, o_ref, *extra_refs, head_cols, t_cols, t_tiles):
    xn = _rms(x_ref[...]) * g_ref[...]
    o = _dot(xn.astype(BF16), w_ref[...])
    o_ref[...] = o
    head_refs, t_refs = extra_refs[:len(head_cols)], extra_refs[len(head_cols):]
    for ref, (col, heads) in zip(head_refs, head_cols):
        for hh in range(heads):
            ref[pl.ds(hh, ROW_TILE, stride=heads), :] = o[:, col + hh * LANES:col + (hh + 1) * LANES]

    @pl.when(pl.program_id(0) < t_tiles)
    def _():
        for ref, (col, width) in zip(t_refs, t_cols):
            ref[0] = o[:, col:col + width].T


def _norm_proj(x, g, w, head_cols=(), t_cols=(), t_rows=0, t_len=ROW_TILE):
    n, d = x.shape
    nout = w.shape[1]
    out_specs = [pl.BlockSpec((ROW_TILE, nout), lambda i: (i, 0))]
    out_shape = [jax.ShapeDtypeStruct((n, nout), F32)]
    for _, heads in head_cols:
        out_specs.append(pl.BlockSpec((ROW_TILE * heads, LANES), lambda i: (i, 0)))
        out_shape.append(jax.ShapeDtypeStruct((n * heads, LANES), F32))
    t_tiles = t_rows // ROW_TILE
    per_seq = t_len // ROW_TILE

    def t_index(i):
        j = jnp.minimum(i, t_tiles - 1)
        return j // per_seq, 0, j % per_seq

    for _, width in t_cols:
        out_specs.append(pl.BlockSpec((1, width, ROW_TILE), t_index))
        out_shape.append(jax.ShapeDtypeStruct((t_rows // t_len, width, t_len), F32))
    return pl.pallas_call(
        functools.partial(_norm_proj_kernel, head_cols=tuple(head_cols), t_cols=tuple(t_cols), t_tiles=t_tiles),
        grid=(n // ROW_TILE,),
        in_specs=[
            pl.BlockSpec((ROW_TILE, d), lambda i: (i, 0)),
            pl.BlockSpec((1, d), lambda i: (0, 0)),
            pl.BlockSpec((d, nout), lambda i: (0, 0)),
        ],
        out_specs=out_specs,
        out_shape=out_shape,
        compiler_params=_params(("arbitrary",)),
        name="norm_proj",
    )(x, g.reshape(1, d), w)


def _route(logits):
    lane = lax.broadcasted_iota(jnp.int32, logits.shape, 1).astype(F32)
    neg = jnp.float32(-jnp.inf)
    big = jnp.float32(LANES)
    gl = jnp.where(lane < MOE_GROUPS, logits, neg)
    gmax = jnp.max(gl, axis=-1, keepdims=True)
    g_idx = jnp.min(jnp.where(gl == gmax, lane, big), axis=-1, keepdims=True)
    g_w = 1.0 / jnp.sum(jnp.exp(gl - gmax), axis=-1, keepdims=True)
    lo = MOE_GROUPS + g_idx * MOE_EXPERTS
    el = jnp.where((lane >= lo) & (lane < lo + MOE_EXPERTS), logits, neg)
    v1 = jnp.max(el, axis=-1, keepdims=True)
    i1 = jnp.min(jnp.where(el == v1, lane, big), axis=-1, keepdims=True)
    el2 = jnp.where(lane == i1, neg, el)
    v2 = jnp.max(el2, axis=-1, keepdims=True)
    i2 = jnp.min(jnp.where(el2 == v2, lane, big), axis=-1, keepdims=True)
    e2 = jnp.exp(v2 - v1)
    w1 = g_w / (1.0 + e2)
    w2 = g_w * e2 / (1.0 + e2)
    ids = jnp.where(lane == 0, i1 - MOE_GROUPS, jnp.where(lane == 1, i2 - MOE_GROUPS, 0.0))
    wts = jnp.where(lane == 0, w1, jnp.where(lane == 1, w2, 0.0))
    return ids.astype(jnp.int32), wts


def _out_route_kernel(*refs, n_a):
    a_refs = refs[:n_a]
    w_refs = refs[n_a:2 * n_a]
    h_ref, g_ref, wrh_ref, wrl_ref, br_ref, ho_ref, xn_ref, ids_ref, wts_ref = refs[2 * n_a:]
    h = h_ref[...]
    for a_ref, w_ref in zip(a_refs, w_refs):
        h = h + _dot(a_ref[...].astype(BF16), w_ref[...])
    ho_ref[...] = h
    xn = _rms(h) * g_ref[...]
    xn_ref[...] = xn
    xh, xl = _split_bf16(xn)
    logits = _dot(xh, wrh_ref[...]) + _dot(xl, wrh_ref[...]) + _dot(xh, wrl_ref[...]) + br_ref[...]
    ids, wts = _route(logits)
    ids_ref[...] = ids
    wts_ref[...] = wts


def _out_route(a_list, w_list, h, g, wr, br):
    n, d = h.shape
    n_a = len(a_list)
    row = lambda i: (i, 0)
    const = lambda i: (0, 0)
    in_specs = [pl.BlockSpec((ROW_TILE, a.shape[1]), row) for a in a_list]
    in_specs += [pl.BlockSpec(w.shape, const) for w in w_list]
    in_specs += [pl.BlockSpec((ROW_TILE, d), row), pl.BlockSpec((1, d), const),
                 pl.BlockSpec((d, LANES), const), pl.BlockSpec((d, LANES), const), pl.BlockSpec((1, LANES), const)]
    wr_hi = wr.astype(BF16)
    wr_lo = (wr - wr_hi.astype(F32)).astype(BF16)
    return pl.pallas_call(
        functools.partial(_out_route_kernel, n_a=n_a),
        grid=(n // ROW_TILE,),
        in_specs=in_specs,
        out_specs=[pl.BlockSpec((ROW_TILE, d), row), pl.BlockSpec((ROW_TILE, d), row),
                   pl.BlockSpec((ROW_TILE, LANES), row), pl.BlockSpec((ROW_TILE, LANES), row)],
        out_shape=[jax.ShapeDtypeStruct((n, d), F32), jax.ShapeDtypeStruct((n, d), F32),
                   jax.ShapeDtypeStruct((n, LANES), jnp.int32), jax.ShapeDtypeStruct((n, LANES), F32)],
        compiler_params=_params(("parallel",)),
        name="out_route",
    )(*a_list, *w_list, h, g.reshape(1, d), wr_hi, wr_lo, br)


def _moe_dispatch_kernel(pos_ref, last_ref, nv_ref, x_ref, xs_hbm, zero_scr, sem):
    i = pl.program_id(0)
    n_tiles = xs_hbm.shape[0] // ROW_TILE

    def zero_tile(t):
        start = pl.multiple_of(t * ROW_TILE, ROW_TILE)
        return pltpu.make_async_copy(zero_scr, xs_hbm.at[pl.ds(start, ROW_TILE), :], sem)

    @pl.when(i == 0)
    def _():
        zero_scr[...] = jnp.zeros(zero_scr.shape, F32)
        for e in range(N_EXPERTS):
            @pl.when(last_ref[e] >= 0)
            def _():
                zero_tile(last_ref[e]).start()

        def fill(t, c):
            zero_tile(t).start()
            return c

        lax.fori_loop(nv_ref[0], n_tiles, fill, 0)
        for e in range(N_EXPERTS):
            @pl.when(last_ref[e] >= 0)
            def _():
                zero_tile(0).wait()

        def drain(t, c):
            zero_tile(0).wait()
            return c

        lax.fori_loop(nv_ref[0], n_tiles, drain, 0)

    def issue(r, c):
        for k in range(MOE_TOPK):
            dst = pos_ref[(i * ROW_TILE + r) * MOE_TOPK + k]
            pltpu.make_async_copy(x_ref.at[pl.ds(r, 1), :], xs_hbm.at[pl.ds(dst, 1), :], sem).start()
        return c

    lax.fori_loop(0, ROW_TILE, issue, 0, unroll=DMA_UNROLL)
    for k in range(MOE_TOPK):
        pltpu.make_async_copy(x_ref, xs_hbm.at[pl.ds(0, ROW_TILE), :], sem).wait()


def _moe_dispatch(xn, pos, last_tile, n_valid, n_rows):
    n, d = xn.shape
    grid_spec = pltpu.PrefetchScalarGridSpec(
        num_scalar_prefetch=3,
        grid=(n // ROW_TILE,),
        in_specs=[pl.BlockSpec((ROW_TILE, d), lambda i, pos, last, nv: (i, 0))],
        out_specs=pl.BlockSpec(memory_space=pl.ANY),
        scratch_shapes=[pltpu.VMEM((ROW_TILE, d), F32), pltpu.SemaphoreType.DMA(())],
    )
    return pl.pallas_call(
        _moe_dispatch_kernel,
        grid_spec=grid_spec,
        out_shape=jax.ShapeDtypeStruct((n_rows, d), F32),
        compiler_params=_params(("arbitrary",)),
        name="moe_dispatch",
    )(pos, last_tile, n_valid, xn)


def _moe_ffn_kernel(te_ref, nv_ref, x_ref, wg_ref, wu_ref, wd_ref, o_ref):
    t = pl.program_id(0)

    @pl.when(t < nv_ref[0])
    def _():
        x = x_ref[...].astype(BF16)
        hid = jax.nn.silu(_dot(x, wg_ref[0].astype(BF16))) * _dot(x, wu_ref[0].astype(BF16))
        o_ref[...] = _dot(hid.astype(BF16), wd_ref[0].astype(BF16))

    @pl.when(t >= nv_ref[0])
    def _():
        o_ref[...] = jnp.zeros_like(o_ref)


def _moe_ffn(x_sorted, tile_expert, n_valid, w_gate, w_up, w_down):
    p, d = x_sorted.shape
    f = w_gate.shape[2]
    used = lambda t, te, nv: (jnp.maximum(jnp.minimum(t, nv[0] - 1), 0), 0)
    grid_spec = pltpu.PrefetchScalarGridSpec(
        num_scalar_prefetch=2,
        grid=(p // ROW_TILE,),
        in_specs=[
            pl.BlockSpec((ROW_TILE, d), used),
            pl.BlockSpec((1, d, f), lambda t, te, nv: (te[t], 0, 0)),
            pl.BlockSpec((1, d, f), lambda t, te, nv: (te[t], 0, 0)),
            pl.BlockSpec((1, f, d), lambda t, te, nv: (te[t], 0, 0)),
        ],
        out_specs=pl.BlockSpec((ROW_TILE, d), lambda t, te, nv: (t, 0)),
    )
    return pl.pallas_call(
        _moe_ffn_kernel,
        grid_spec=grid_spec,
        out_shape=jax.ShapeDtypeStruct((p, d), F32),
        compiler_params=_params(("arbitrary",)),
        name="moe_ffn",
    )(tile_expert, n_valid, x_sorted, w_gate, w_up, w_down)


def _moe_combine_kernel(pos_ref, h_ref, w_ref, *refs, final_norm):
    g_ref = refs[0] if final_norm else None
    y_hbm, o_ref, rbuf, sem = refs[-4:]
    i = pl.program_id(0)

    def issue(r, c):
        for k in range(MOE_TOPK):
            src = pos_ref[(i * ROW_TILE + r) * MOE_TOPK + k]
            pltpu.make_async_copy(y_hbm.at[pl.ds(src, 1), :], rbuf.at[k, pl.ds(r, 1), :], sem).start()
        return c

    lax.fori_loop(0, ROW_TILE, issue, 0, unroll=DMA_UNROLL)
    for k in range(MOE_TOPK):
        pltpu.make_async_copy(y_hbm.at[pl.ds(0, ROW_TILE), :], rbuf.at[k], sem).wait()
    w = w_ref[...]
    acc = h_ref[...]
    for k in range(MOE_TOPK):
        acc = acc + w[:, k:k + 1] * rbuf[k]
    if final_norm:
        acc = _rms(acc) * g_ref[...]
    o_ref[...] = acc


def _moe_combine(h, wts, y_sorted, pos, gain=None):
    n, d = h.shape
    row = lambda i, pos: (i, 0)
    in_specs = [pl.BlockSpec((ROW_TILE, d), row), pl.BlockSpec((ROW_TILE, LANES), row)]
    args = [h, wts]
    if gain is not None:
        in_specs.append(pl.BlockSpec((1, d), lambda i, pos: (0, 0)))
        args.append(gain.reshape(1, d))
    grid_spec = pltpu.PrefetchScalarGridSpec(
        num_scalar_prefetch=1,
        grid=(n // ROW_TILE,),
        in_specs=in_specs + [pl.BlockSpec(memory_space=pl.ANY)],
        out_specs=pl.BlockSpec((ROW_TILE, d), row),
        scratch_shapes=[pltpu.VMEM((MOE_TOPK, ROW_TILE, d), F32), pltpu.SemaphoreType.DMA(())],
    )
    return pl.pallas_call(
        functools.partial(_moe_combine_kernel, final_norm=gain is not None),
        grid_spec=grid_spec,
        out_shape=jax.ShapeDtypeStruct((n, d), F32),
        compiler_params=_params(("arbitrary",)),
        name="moe_combine",
    )(pos, *args, y_sorted)


def _dispatch_plan(ids):
    n = ids.shape[0]
    a = n * MOE_TOPK
    n_tiles = a // ROW_TILE + N_EXPERTS
    e_ids = jnp.arange(N_EXPERTS, dtype=jnp.int32)
    onehot = (ids.reshape(a, 1) == e_ids[None, :]).astype(jnp.int32)
    seen = jnp.cumsum(onehot, axis=0)
    counts = seen[-1]
    tiles_per = (counts + ROW_TILE - 1) // ROW_TILE
    tile_end = jnp.cumsum(tiles_per)
    tile_start = tile_end - tiles_per
    pos = jnp.sum(onehot * (seen - 1 + (tile_start * ROW_TILE)[None, :]), axis=1).astype(jnp.int32)
    n_valid = tile_end[-1].astype(jnp.int32)
    t_idx = jnp.arange(n_tiles, dtype=jnp.int32)
    te = jnp.minimum(jnp.sum((tile_end[None, :] <= t_idx[:, None]).astype(jnp.int32), axis=1), N_EXPERTS - 1)
    last = jnp.max(jnp.where(tiles_per > 0, e_ids, 0))
    tile_expert = jnp.where(t_idx < n_valid, te, last).astype(jnp.int32)
    last_tile = jnp.where(tiles_per > 0, tile_end - 1, -1).astype(jnp.int32)
    return tile_expert, n_valid.reshape(1), pos, last_tile, n_tiles * ROW_TILE


def _moe(h, xn, ids, wts, w_gate, w_up, w_down, final_gain=None):
    tile_expert, n_valid, pos, last_tile, n_rows = _dispatch_plan(ids[:, :MOE_TOPK])
    x_sorted = _moe_dispatch(xn, pos, last_tile, n_valid, n_rows)
    y_sorted = _moe_ffn(x_sorted, tile_expert, n_valid, w_gate, w_up, w_down)
    return _moe_combine(h, wts, y_sorted, pos, final_gain)


def _lambda_from(lam_ref):
    lp = lam_ref[...]
    s1 = jnp.sum(lp[0:1, :] * lp[1:2, :], axis=-1, keepdims=True)
    s2 = jnp.sum(lp[2:3, :] * lp[3:4, :], axis=-1, keepdims=True)
    return jnp.exp(s1) - jnp.exp(s2) + LAMBDA_INIT


def _transpose_tiles(src_ref, dst_ref, tile):
    def body(j, c):
        start = pl.multiple_of(j * tile, tile)
        dst_ref[j] = src_ref[pl.ds(start, tile), :].T.astype(BF16)
        return c

    lax.fori_loop(0, dst_ref.shape[0], body, 0)


def _diff_prompt_kernel(slopes_ref, q_ref, k_ref, v_ref, lam_ref, g_ref, o_ref, vt_scr, s_scr, m_scr, l_scr, acc_scr,
                        *, tq, tk):
    h = pl.program_id(1)
    qi = pl.program_id(2)

    @pl.when(qi == 0)
    def _():
        _transpose_tiles(v_ref, vt_scr, tk)

    slope = slopes_ref[h] * LOG2E
    q = q_ref[...] * (A_DH ** -0.5 * LOG2E)
    lane = lax.broadcasted_iota(jnp.int32, q.shape, 1)
    qcat = jnp.concatenate([jnp.where(lane < A_DH, q, 0.0), jnp.where(lane >= A_DH, q, 0.0)], axis=0).astype(BF16)
    key = lax.broadcasted_iota(jnp.int32, (tk, 2 * tq), 0)
    qry = lax.broadcasted_iota(jnp.int32, (tk, 2 * tq), 1) % tq
    bias0 = slope * key.astype(F32)
    m_scr[...] = jnp.full(m_scr.shape, -jnp.inf, F32)
    l_scr[...] = jnp.zeros(l_scr.shape, F32)
    acc_scr[...] = jnp.zeros(acc_scr.shape, F32)

    def scores(j):
        start = pl.multiple_of(j * tk, tk)
        return _dot_nt(k_ref[pl.ds(start, tk), :].astype(BF16), qcat)

    def update(slot, j, mask):
        cb = slope * (j * tk).astype(F32)
        t = s_scr[slot] + bias0
        if mask is not None:
            t = jnp.where(mask, t, -jnp.inf)
        m_old = m_scr[...]
        m_new = jnp.maximum(m_old, jnp.max(t, axis=0, keepdims=True) + cb)
        p = jnp.exp2(t + (cb - m_new))
        alpha = jnp.exp2(m_old - m_new)
        l_scr[...] = alpha * l_scr[...] + jnp.sum(p, axis=0, keepdims=True)
        acc_scr[...] = alpha * acc_scr[...] + _dot(vt_scr[j], p.astype(BF16))
        m_scr[...] = m_new

    s_scr[0] = scores(0)

    def body(i, carry):
        j = 2 * i
        s_scr[1] = scores(j + 1)
        update(0, j, None)
        s_scr[0] = scores(j + 2)
        update(1, j + 1, None)
        return carry

    lax.fori_loop(0, qi // 2, body, 0)
    diag = key <= qry

    @pl.when(qi % 2 == 1)
    def _():
        s_scr[1] = scores(qi)
        update(0, qi - 1, None)
        update(1, qi, diag)

    @pl.when(qi % 2 == 0)
    def _():
        update(0, qi, diag)

    lam = _lambda_from(lam_ref)
    o = acc_scr[...] / l_scr[...]
    o = (o[:, :tq] - lam * o[:, tq:]).T
    o_ref[...] = _rms(o) * g_ref[...] * (1.0 - LAMBDA_INIT)


def _diff_prompt(proj, slopes, lam_pack, subln, n_batch, t_len):
    tq = tk = min(ATT_TILE_A, t_len)
    nq = t_len // tq
    qcol, kcol, vcol = 0, A_HEADS, 2 * A_HEADS
    grid_spec = pltpu.PrefetchScalarGridSpec(
        num_scalar_prefetch=1,
        grid=(n_batch, A_HEADS, nq),
        in_specs=[
            pl.BlockSpec((tq, LANES), lambda b, h, i, s: (b * nq + i, qcol + h)),
            pl.BlockSpec((t_len, LANES), lambda b, h, i, s: (b, kcol + h)),
            pl.BlockSpec((t_len, LANES), lambda b, h, i, s: (b, vcol + h)),
            pl.BlockSpec((8, LANES), lambda b, h, i, s: (0, 0)),
            pl.BlockSpec((1, LANES), lambda b, h, i, s: (0, 0)),
        ],
        out_specs=pl.BlockSpec((tq, LANES), lambda b, h, i, s: (b * nq + i, h)),
        scratch_shapes=[pltpu.VMEM((t_len // tk, LANES, tk), BF16), pltpu.VMEM((2, tk, 2 * tq), F32),
                        pltpu.VMEM((1, 2 * tq), F32), pltpu.VMEM((1, 2 * tq), F32), pltpu.VMEM((LANES, 2 * tq), F32)],
    )
    return pl.pallas_call(
        functools.partial(_diff_prompt_kernel, tq=tq, tk=tk),
        grid_spec=grid_spec,
        out_shape=jax.ShapeDtypeStruct((n_batch * t_len, A_HEADS * A_DV), F32),
        compiler_params=_params(("parallel", "parallel", "arbitrary")),
        name="diff_prompt",
    )(slopes, proj, proj, proj, lam_pack, subln.reshape(1, LANES))


def _diff_sample_kernel(pt_ref, slopes_ref, q_ref, kn_ref, vn_ref, *refs, ts, page, n_pages, group):
    kp_refs, vp_refs = refs[:group], refs[group:2 * group]
    lam_ref, g_ref, o_ref, m_scr, l_scr, acc_scr = refs[2 * group:]
    p = pl.program_id(1)
    hrows = 2 * ts
    rows = A_HEADS * hrows
    past = n_pages * page
    q = q_ref[...] * (A_DH ** -0.5 * LOG2E)
    lane = lax.broadcasted_iota(jnp.int32, (ts, LANES), 1)
    qh = []
    for hh in range(A_HEADS):
        qq = q[:, hh * LANES:(hh + 1) * LANES]
        qh.append(jnp.concatenate([jnp.where(lane < A_DH, qq, 0.0), jnp.where(lane >= A_DH, qq, 0.0)], axis=0).astype(BF16))
    rid = lax.broadcasted_iota(jnp.int32, (rows, 1), 0)
    rt = rid % ts
    slope_rows = jnp.zeros((rows, 1), F32)
    for hh in range(A_HEADS):
        slope_rows = jnp.where(rid // hrows == hh, slopes_ref[hh] * LOG2E, slope_rows)

    def update(k_of, v_of, kpos, new_tokens):
        s = jnp.concatenate([jnp.concatenate([_dot_nt(qh[hh], k.astype(BF16)) for k in k_of(hh)], axis=1)
                             for hh in range(A_HEADS)], axis=0)
        t = s + slope_rows * kpos.astype(F32)
        if new_tokens:
            t = jnp.where(kpos - past <= rt, t, -jnp.inf)
        m_old = m_scr[...]
        m_new = jnp.maximum(m_old, jnp.max(t, axis=-1, keepdims=True))
        pr = jnp.exp2(t - m_new)
        alpha = jnp.exp2(m_old - m_new)
        l_scr[...] = alpha * l_scr[...] + jnp.sum(pr, axis=-1, keepdims=True)
        for hh in range(A_HEADS):
            sl = slice(hh * hrows, (hh + 1) * hrows)
            pv, off = 0.0, 0
            for v in v_of(hh):
                pv = pv + _dot(pr[sl, off:off + v.shape[0]].astype(BF16), v.astype(BF16))
                off += v.shape[0]
            acc_scr[hh] = alpha[sl] * acc_scr[hh] + pv
        m_scr[...] = m_new

    @pl.when(p == 0)
    def _():
        m_scr[...] = jnp.full(m_scr.shape, -jnp.inf, F32)
        l_scr[...] = jnp.zeros(l_scr.shape, F32)
        acc_scr[...] = jnp.zeros(acc_scr.shape, F32)
        kpos = past + lax.broadcasted_iota(jnp.int32, (1, ts), 1)
        update(lambda hh: [kn_ref[:, hh * LANES:(hh + 1) * LANES]], lambda hh: [vn_ref[:, hh * A_DV:(hh + 1) * A_DV]],
               kpos, True)

    kpos = p * (group * page) + lax.broadcasted_iota(jnp.int32, (1, group * page), 1)
    update(lambda hh: [r[0, pl.ds(hh, page, stride=A_HEADS), :] for r in kp_refs],
           lambda hh: [r[0, pl.ds(hh, page, stride=A_HEADS), :] for r in vp_refs], kpos, False)

    @pl.when(p == n_pages // group - 1)
    def _():
        lam = _lambda_from(lam_ref)
        g = g_ref[...]
        inv_l = 1.0 / l_scr[...]
        outs = []
        for hh in range(A_HEADS):
            o = acc_scr[hh] * inv_l[hh * hrows:(hh + 1) * hrows]
            o = o[:ts] - lam * o[ts:]
            outs.append(_rms(o) * g * (1.0 - LAMBDA_INIT))
        o_ref[...] = jnp.concatenate(outs, axis=-1)


def _diff_sample(proj, row0, cache_k, cache_v, page_table, slopes, lam_pack, subln, n_seq, ts):
    n_pages = page_table.shape[1]
    page = cache_k.shape[1]
    width = A_HEADS * 2 * A_DH
    vwidth = A_HEADS * A_DV
    rb = row0 // ts
    rows = 2 * A_HEADS * ts
    qblk, kblk, vblk = 0, 1, 2
    group = math.gcd(n_pages, PAGE_GROUP)

    def page_spec(i, w):
        return pl.BlockSpec((1, page * A_HEADS, w), lambda s, p, pt, sl: (pt[s, p * group + i], 0, 0))

    grid_spec = pltpu.PrefetchScalarGridSpec(
        num_scalar_prefetch=2,
        grid=(n_seq, n_pages // group),
        in_specs=[
            pl.BlockSpec((ts, width), lambda s, p, pt, sl: (rb + s, qblk)),
            pl.BlockSpec((ts, width), lambda s, p, pt, sl: (rb + s, kblk)),
            pl.BlockSpec((ts, vwidth), lambda s, p, pt, sl: (rb + s, vblk)),
            *[page_spec(i, 2 * A_DH) for i in range(group)],
            *[page_spec(i, A_DV) for i in range(group)],
            pl.BlockSpec((8, LANES), lambda s, p, pt, sl: (0, 0)),
            pl.BlockSpec((1, LANES), lambda s, p, pt, sl: (0, 0)),
        ],
        out_specs=pl.BlockSpec((ts, vwidth), lambda s, p, pt, sl: (s, 0)),
        scratch_shapes=[pltpu.VMEM((rows, 1), F32), pltpu.VMEM((rows, 1), F32),
                        pltpu.VMEM((A_HEADS, 2 * ts, A_DV), F32)],
    )
    k_pages = cache_k.reshape(-1, page * A_HEADS, 2 * A_DH)
    v_pages = cache_v.reshape(-1, page * A_HEADS, A_DV)
    return pl.pallas_call(
        functools.partial(_diff_sample_kernel, ts=ts, page=page, n_pages=n_pages, group=group),
        grid_spec=grid_spec,
        out_shape=jax.ShapeDtypeStruct((n_seq * ts, vwidth), F32),
        compiler_params=_params(("parallel", "arbitrary")),
        name="diff_sample",
    )(page_table, slopes, proj, proj, proj, *([k_pages] * group), *([v_pages] * group), lam_pack,
      subln.reshape(1, LANES))


def _retention_kernel(q_ref, k_ref, v_ref, g_ref, cos_ref, sin_ref, s0_ref, o_ref, s_out_ref, s_scr, *, chunk):
    c = pl.program_id(1)

    @pl.when(c == 0)
    def _():
        s_scr[...] = s0_ref[0]

    cos = cos_ref[...]
    sin = sin_ref[...]
    width = B_HEADS * B_DK
    lane = lax.broadcasted_iota(jnp.int32, (chunk, width), 1)
    first_half = (lane % B_DK) < (B_DK // 2)

    def rope(x):
        partner = jnp.where(first_half, pltpu.roll(x, width - B_DK // 2, 1), pltpu.roll(x, B_DK // 2, 1))
        return x * cos + partner * sin

    q = rope(q_ref[...])
    k = rope(k_ref[...]) * (B_DK ** -0.5)
    v = v_ref[...]
    g = g_ref[...]
    n_col = lax.broadcasted_iota(jnp.int32, (chunk, 1), 0).astype(F32)
    ri = lax.broadcasted_iota(jnp.int32, (chunk, chunk), 0)
    ci = lax.broadcasted_iota(jnp.int32, (chunk, chunk), 1)
    dist = (ri - ci).astype(F32)
    outs = []
    for hh in range(B_HEADS):
        log_g = math.log1p(-(2.0 ** (-5.0 - hh)))
        decay = jnp.where(ri >= ci, jnp.exp(log_g * jnp.maximum(dist, 0.0)), 0.0)
        qh = q[:, hh * B_DK:(hh + 1) * B_DK]
        kh = k[:, hh * B_DK:(hh + 1) * B_DK]
        vh = v[:, hh * B_DV:(hh + 1) * B_DV].astype(BF16)
        s_h = s_scr[hh]
        inner = _dot_nt(qh.astype(BF16), kh.astype(BF16)) * decay
        q_dec = qh * jnp.exp(log_g * (n_col + 1.0))
        o = _dot(inner.astype(BF16), vh) + _dot(q_dec.astype(BF16), s_h.astype(BF16))
        k_dec = kh * jnp.exp(log_g * (chunk - 1.0 - n_col))
        s_scr[hh] = math.exp(log_g * chunk) * s_h + _dot_tn(k_dec.astype(BF16), vh)
        gh = g[:, hh * B_DV:(hh + 1) * B_DV]
        outs.append(jax.nn.silu(gh) * _rms(o))
    o_ref[...] = jnp.concatenate(outs, axis=-1)

    @pl.when(c == pl.num_programs(1) - 1)
    def _():
        s_out_ref[0] = s_scr[...]


def _retention(proj, row0, cos_t, sin_t, s0, n_seq, t_len):
    chunk = CHUNK if (t_len > CHUNK and t_len % CHUNK == 0) else t_len
    nc = t_len // chunk
    rb = row0 // chunk
    qw, vw = B_HEADS * B_DK, B_HEADS * B_DV
    qcol, kcol, vcol, gcol = 6, 7, 4, 5
    row = lambda s, c: (rb + s * nc + c, 0)
    return pl.pallas_call(
        functools.partial(_retention_kernel, chunk=chunk),
        grid=(n_seq, nc),
        in_specs=[
            pl.BlockSpec((chunk, qw), lambda s, c: (rb + s * nc + c, qcol)),
            pl.BlockSpec((chunk, qw), lambda s, c: (rb + s * nc + c, kcol)),
            pl.BlockSpec((chunk, vw), lambda s, c: (rb + s * nc + c, vcol)),
            pl.BlockSpec((chunk, vw), lambda s, c: (rb + s * nc + c, gcol)),
            pl.BlockSpec((chunk, qw), lambda s, c: (c, 0)),
            pl.BlockSpec((chunk, qw), lambda s, c: (c, 0)),
            pl.BlockSpec((1, B_HEADS, B_DK, B_DV), lambda s, c: (s, 0, 0, 0)),
        ],
        out_specs=[pl.BlockSpec((chunk, vw), lambda s, c: (s * nc + c, 0)),
                   pl.BlockSpec((1, B_HEADS, B_DK, B_DV), lambda s, c: (s, 0, 0, 0))],
        out_shape=[jax.ShapeDtypeStruct((n_seq * t_len, vw), F32),
                   jax.ShapeDtypeStruct((n_seq, B_HEADS, B_DK, B_DV), F32)],
        scratch_shapes=[pltpu.VMEM((B_HEADS, B_DK, B_DV), F32)],
        compiler_params=_params(("parallel", "arbitrary")),
        name="retention",
    )(proj, proj, proj, proj, cos_t, sin_t, s0)


def _rope_tables(pos):
    half = B_DK // 2
    inv = ROPE_BASE ** (-jnp.arange(half, dtype=F32) / half)
    ang = pos.astype(F32)[:, None] * inv[None, :]
    cos, sin = jnp.cos(ang), jnp.sin(ang)
    cos_t = jnp.tile(jnp.concatenate([cos, cos], axis=-1), (1, B_HEADS))
    sin_t = jnp.tile(jnp.concatenate([-sin, sin], axis=-1), (1, B_HEADS))
    return cos_t, sin_t


def _softplus2(z2):
    return jnp.maximum(z2, 0.0) + jnp.log2(1.0 + jnp.exp2(-jnp.abs(z2)))


def _split_bf16(x):
    hi = x.astype(BF16)
    return hi, (x - hi.astype(F32)).astype(BF16)


def _stick_weights_t(zt, carry, newer_mat, mask):
    sp = _softplus2(zt)
    lk = -sp
    if mask is not None:
        lk = jnp.where(mask, lk, 0.0)
    hi, lo = _split_bf16(lk)
    newer = _dot(newer_mat, hi) + _dot(newer_mat, lo)
    a = jnp.exp2((zt - sp) + newer + carry)
    if mask is not None:
        a = jnp.where(mask, a, 0.0)
    return a, carry + jnp.sum(lk, axis=0, keepdims=True)


def _stick_prompt_kernel(q_ref, k_ref, v_ref, o_ref, vt_scr, carry_scr, acc_scr, *, tq, tk):
    qi = pl.program_id(2)

    @pl.when(qi == 0)
    def _():
        _transpose_tiles(v_ref, vt_scr, tk)

    heads = LANES // C_DH
    width = heads * tq
    q = q_ref[...] * (C_DH ** -0.5 * LOG2E)
    lane = lax.broadcasted_iota(jnp.int32, q.shape, 1)
    qcat = jnp.concatenate([jnp.where(lane // C_DH == hh, q, 0.0) for hh in range(heads)], axis=0).astype(BF16)
    key = lax.broadcasted_iota(jnp.int32, (tk, width), 0)
    qry = lax.broadcasted_iota(jnp.int32, (tk, width), 1) % tq
    rj = lax.broadcasted_iota(jnp.int32, (tk, tk), 0)
    cs = lax.broadcasted_iota(jnp.int32, (tk, tk), 1)
    newer_mat = jnp.where(cs > rj, 1.0, 0.0).astype(BF16)
    sub = tq // tk

    def block(j, mask, carry):
        start = pl.multiple_of(j * tk, tk)
        zt = _dot_nt(k_ref[pl.ds(start, tk), :].astype(BF16), qcat)
        a, carry = _stick_weights_t(zt, carry, newer_mat, mask)
        return _dot(vt_scr[j], a.astype(BF16)), carry

    carry = jnp.zeros((1, width), F32)
    for jj in reversed(range(sub)):
        av, carry = block(qi * sub + jj, (jj * tk + key) < qry, carry)
        if jj == sub - 1:
            acc_scr[...] = av
        else:
            acc_scr[...] += av
    carry_scr[...] = carry

    def cond(state):
        j, cmax = state
        return (j >= 0) & (cmax > STICK_DEAD)

    def body(state):
        j, _ = state
        av, carry = block(j, None, carry_scr[...])
        acc_scr[...] += av
        carry_scr[...] = carry
        return j - 1, jnp.max(carry)

    lax.while_loop(cond, body, (qi * sub - 1, jnp.max(carry)))
    acc = acc_scr[...]
    o_t = jnp.concatenate([acc[hh * C_DH:(hh + 1) * C_DH, hh * tq:(hh + 1) * tq] for hh in range(heads)], axis=0)
    o_ref[...] = o_t.T


def _stick_prompt(proj, n_batch, t_len):
    tq = min(ATT_TILE_CQ, t_len)
    tk = min(ATT_TILE_CK, tq)
    nq = t_len // tq
    ng = C_HEADS * C_DH // LANES
    return pl.pallas_call(
        functools.partial(_stick_prompt_kernel, tq=tq, tk=tk),
        grid=(n_batch, ng, nq),
        in_specs=[
            pl.BlockSpec((tq, LANES), lambda b, g, i: (b * nq + i, g)),
            pl.BlockSpec((t_len, LANES), lambda b, g, i: (b, ng + g)),
            pl.BlockSpec((t_len, LANES), lambda b, g, i: (b, 2 * ng + g)),
        ],
        out_specs=pl.BlockSpec((tq, LANES), lambda b, g, i: (b * nq + i, g)),
        out_shape=jax.ShapeDtypeStruct((n_batch * t_len, C_HEADS * C_DH), F32),
        scratch_shapes=[pltpu.VMEM((t_len // tk, LANES, tk), BF16), pltpu.VMEM((1, LANES // C_DH * tq), F32),
                        pltpu.VMEM((LANES, LANES // C_DH * tq), F32)],
        compiler_params=_params(("parallel", "parallel", "arbitrary")),
        name="stick_prompt",
    )(proj, proj, proj)


def _stick_sample_kernel(pt_ref, q_ref, kn_ref, vn_ref, ck_hbm, cv_hbm, o_ref, kbuf, vbuf, ksem, vsem,
                         carry_scr, acc_scr, *, ts, page, n_pages):
    s = pl.program_id(0)
    rows = C_HEADS * ts

    def copies(seq, i, slot):
        pid = pt_ref[seq, n_pages - 1 - i]
        return (pltpu.make_async_copy(ck_hbm.at[pid], kbuf.at[slot], ksem.at[slot]),
                pltpu.make_async_copy(cv_hbm.at[pid], vbuf.at[slot], vsem.at[slot]))

    def start(seq, i, slot):
        for cp in copies(seq, i, slot):
            cp.start()

    def wait(seq, i, slot):
        for cp in copies(seq, i, slot):
            cp.wait()

    @pl.when(s == 0)
    def _():
        start(0, 0, 0)
        start(0, 1, 1)

    q = q_ref[...] * (C_DH ** -0.5 * LOG2E)
    qh = [q[:, hh * C_DH:(hh + 1) * C_DH].astype(BF16) for hh in range(C_HEADS)]
    r = lax.broadcasted_iota(jnp.int32, (page, page), 0)
    c = lax.broadcasted_iota(jnp.int32, (page, page), 1)
    upper = jnp.where(r > c, 1.0, 0.0).astype(BF16)

    def tile(z_of, av_of, carry, mask):
        z = jnp.concatenate([z_of(hh) for hh in range(C_HEADS)], axis=0)
        sp = _softplus2(z)
        lk = -sp
        if mask is not None:
            lk = jnp.where(mask, lk, 0.0)
        hi, lo = _split_bf16(lk)
        newer = _dot(hi, upper) + _dot(lo, upper)
        a = jnp.exp2((z - sp) + newer + carry)
        if mask is not None:
            a = jnp.where(mask, a, 0.0)
        for hh in range(C_HEADS):
            acc_scr[hh] += av_of(hh, a[hh * ts:(hh + 1) * ts].astype(BF16))
        return carry + jnp.sum(lk, axis=-1, keepdims=True)

    acc_scr[...] = jnp.zeros(acc_scr.shape, F32)
    pad = jnp.zeros((page - ts, C_DH), F32)
    rt = lax.broadcasted_iota(jnp.int32, (rows, page), 0) % ts
    kc = lax.broadcasted_iota(jnp.int32, (rows, page), 1)

    def new_rows(ref, hh):
        return jnp.concatenate([ref[:, hh * C_DH:(hh + 1) * C_DH], pad], axis=0).astype(BF16)

    carry = tile(lambda hh: _dot_nt(qh[hh], new_rows(kn_ref, hh)), lambda hh, a: _dot(a, new_rows(vn_ref, hh)),
                 jnp.zeros((rows, 1), F32), kc < rt)

    def page_tile(slot, carry):
        return tile(lambda hh: _dot(qh[hh], kbuf[slot, hh].astype(BF16)),
                    lambda hh, a: _dot_nt(a, vbuf[slot, hh].astype(BF16)), carry, None)

    wait(s, 0, 0)
    carry = page_tile(0, carry)
    cmax = jnp.max(carry)
    carry_scr[...] = carry

    @pl.when((2 < n_pages) & (cmax > STICK_DEAD))
    def _():
        start(s, 2, 0)

    def cond(state):
        i, cmax = state
        return (i < n_pages) & (cmax > STICK_DEAD)

    def body(state):
        i, _ = state
        slot = i % 2
        wait(s, i, slot)
        carry = page_tile(slot, carry_scr[...])
        cmax = jnp.max(carry)
        carry_scr[...] = carry

        @pl.when((i + 2 < n_pages) & (cmax > STICK_DEAD))
        def _():
            start(s, i + 2, slot)

        return i + 1, cmax

    i_end, _ = lax.while_loop(cond, body, (jnp.int32(1), cmax))

    @pl.when(i_end < n_pages)
    def _():
        wait(s, i_end, i_end % 2)

    @pl.when(s + 1 < pl.num_programs(0))
    def _():
        start(s + 1, 0, 0)
        start(s + 1, 1, 1)

    o_ref[0] = acc_scr[...]


def _stick_sample(proj, row0, cache_k, cache_v, page_table, n_seq, ts):
    n_pages = page_table.shape[1]
    page = cache_k.shape[1]
    assert n_pages >= 2
    width = C_HEADS * C_DH
    rb = row0 // ts
    grid_spec = pltpu.PrefetchScalarGridSpec(
        num_scalar_prefetch=1,
        grid=(n_seq,),
        in_specs=[
            pl.BlockSpec((ts, width), lambda s, pt: (rb + s, 0)),
            pl.BlockSpec((ts, width), lambda s, pt: (rb + s, 1)),
            pl.BlockSpec((ts, width), lambda s, pt: (rb + s, 2)),
            pl.BlockSpec(memory_space=pl.ANY),
            pl.BlockSpec(memory_space=pl.ANY),
        ],
        out_specs=pl.BlockSpec((1, C_HEADS, ts, C_DH), lambda s, pt: (s, 0, 0, 0)),
        scratch_shapes=[pltpu.VMEM((2, C_HEADS, C_DH, page), F32), pltpu.VMEM((2, C_HEADS, C_DH, page), F32),
                        pltpu.SemaphoreType.DMA((2,)), pltpu.SemaphoreType.DMA((2,)),
                        pltpu.VMEM((C_HEADS * ts, 1), F32), pltpu.VMEM((C_HEADS, ts, C_DH), F32)],
    )
    out = pl.pallas_call(
        functools.partial(_stick_sample_kernel, ts=ts, page=page, n_pages=n_pages),
        grid_spec=grid_spec,
        out_shape=jax.ShapeDtypeStruct((n_seq, C_HEADS, ts, C_DH), F32),
        compiler_params=_params(("arbitrary",)),
        name="stick_sample",
    )(page_table, proj, proj, proj, jnp.transpose(cache_k, (0, 2, 3, 1)), jnp.transpose(cache_v, (0, 2, 3, 1)))
    return jnp.transpose(out, (0, 2, 1, 3)).reshape(n_seq * ts, width)


def _router_pack(wg, bg, we, be):
    d = wg.shape[0]
    we_flat = jnp.transpose(we, (1, 0, 2)).reshape(d, N_EXPERTS)
    w = jnp.concatenate([wg, we_flat], axis=1)
    b = jnp.concatenate([bg, be.reshape(N_EXPERTS)])
    pad = LANES - w.shape[1]
    return jnp.pad(w, ((0, 0), (0, pad))), jnp.pad(b, (0, pad)).reshape(1, LANES)


def _expert_weights(w_gate, w_up, w_down):
    d, f = w_gate.shape[2], w_gate.shape[3]
    return w_gate.reshape(N_EXPERTS, d, f), w_up.reshape(N_EXPERTS, d, f), w_down.reshape(N_EXPERTS, f, d)


def kernel(x_prompt, x_sample, cache_a_k, cache_a_v, state_b, cache_c_k, cache_c_v, page_table, ln0_mix, w_in0, lam_q1, lam_k1, lam_q2, lam_k2, a_subln, w_out0, ln0_ffn, moe0_wg, moe0_bg, moe0_we, moe0_be, moe0_w_gate, moe0_w_up, moe0_w_down, ln1_mix, w_in1, w_out1, ln1_ffn, moe1_wg, moe1_bg, moe1_we, moe1_be, moe1_w_gate, moe1_w_up, moe1_w_down, ln_f):
    bp, tp, d = x_prompt.shape
    db, ts, _ = x_sample.shape
    n_p, n_s = bp * tp, db * ts
    past = page_table.shape[1] * cache_a_k.shape[1]
    h = jnp.concatenate([x_prompt.reshape(n_p, d), x_sample.reshape(n_s, d)], axis=0)

    slopes = jnp.exp2(-8.0 * jnp.arange(1, A_HEADS + 1, dtype=F32) / A_HEADS)
    lam_pack = jnp.pad(jnp.stack([lam_q1, lam_k1, lam_q2, lam_k2]), ((0, 4), (0, LANES - A_DH)))
    cos_p, sin_p = _rope_tables(jnp.arange(tp, dtype=jnp.int32))
    cos_s, sin_s = _rope_tables(past + jnp.arange(ts, dtype=jnp.int32))

    a_w = A_HEADS * 2 * A_DH
    proj0, a_k_rows, a_v_rows = _norm_proj(h, ln0_mix, w_in0.astype(BF16), ((a_w, A_HEADS), (2 * a_w, A_HEADS)))
    a_o_p = _diff_prompt(proj0, slopes, lam_pack, a_subln, bp, tp)
    a_o_s = _diff_sample(proj0, n_p, cache_a_k, cache_a_v, page_table, slopes, lam_pack, a_subln, db, ts)
    b_o_p, b_state_p = _retention(proj0, 0, cos_p, sin_p, jnp.zeros((bp, B_HEADS, B_DK, B_DV), F32), bp, tp)
    b_o_s, b_state_s = _retention(proj0, n_p, cos_s, sin_s, state_b.astype(F32), db, ts)
    a_o = jnp.concatenate([a_o_p, a_o_s], axis=0)
    b_o = jnp.concatenate([b_o_p, b_o_s], axis=0)
    w_out0_b = w_out0.astype(BF16)
    wr0, br0 = _router_pack(moe0_wg, moe0_bg, moe0_we, moe0_be)
    h, xn, ids, wts = _out_route([a_o, b_o], [w_out0_b[:A_HEADS * A_DV], w_out0_b[A_HEADS * A_DV:]], h, ln0_ffn, wr0, br0)
    h = _moe(h, xn, ids, wts, *_expert_weights(moe0_w_gate, moe0_w_up, moe0_w_down))

    c_w = C_HEADS * C_DH
    proj1, c_k_t, c_v_t = _norm_proj(h, ln1_mix, w_in1.astype(BF16), t_cols=((c_w, c_w), (2 * c_w, c_w)),
                                     t_rows=n_p, t_len=tp)
    c_o_p = _stick_prompt(proj1, bp, tp)
    c_o_s = _stick_sample(proj1, n_p, cache_c_k, cache_c_v, page_table, db, ts)
    c_o = jnp.concatenate([c_o_p, c_o_s], axis=0)
    wr1, br1 = _router_pack(moe1_wg, moe1_bg, moe1_we, moe1_be)
    h, xn, ids, wts = _out_route([c_o], [w_out1.astype(BF16)], h, ln1_ffn, wr1, br1)
    y = _moe(h, xn, ids, wts, *_expert_weights(moe1_w_gate, moe1_w_up, moe1_w_down), final_gain=ln_f)

    def heads_last(a_t):
        return jnp.transpose(a_t.reshape(bp, C_HEADS, C_DH, tp), (0, 3, 1, 2))

    y_p, y_s = y[:n_p].reshape(bp, tp, d), y[n_p:].reshape(db, ts, d)
    a_k_p, a_k_s = a_k_rows[:n_p * A_HEADS].reshape(bp, tp, A_HEADS, 2 * A_DH), a_k_rows[n_p * A_HEADS:].reshape(db, ts, A_HEADS, 2 * A_DH)
    a_v_p, a_v_s = a_v_rows[:n_p * A_HEADS].reshape(bp, tp, A_HEADS, A_DV), a_v_rows[n_p * A_HEADS:].reshape(db, ts, A_HEADS, A_DV)
    c_k_p, c_v_p = heads_last(c_k_t), heads_last(c_v_t)
    c_k_s = proj1[n_p:, c_w:2 * c_w].reshape(db, ts, C_HEADS, C_DH)
    c_v_s = proj1[n_p:, 2 * c_w:].reshape(db, ts, C_HEADS, C_DH)
    return (y_p, y_s, a_k_p, a_v_p, b_state_p, c_k_p, c_v_p, a_k_s, a_v_s, b_state_s, c_k_s, c_v_s)
```

```python
import functools
import math

import jax
import jax.numpy as jnp
from jax import lax
from jax.experimental import pallas as pl
from jax.experimental.pallas import tpu as pltpu

F32 = jnp.float32
BF16 = jnp.bfloat16

D_MODEL = 1024
EPS = 1e-6
A_HEADS, A_DH, A_DV = 4, 64, 128
LAMBDA_INIT = 0.8 - 0.6 * math.exp(-0.3 * 0)
B_HEADS, B_DK, B_DV = 4, 64, 128
CHUNK = 128
ROPE_BASE = 10000.0
C_HEADS, C_DH = 16, 64
MOE_GROUPS, MOE_EXPERTS, MOE_TOPK = 4, 8, 2
MOE_FF = D_MODEL // 2
N_EXPERTS = MOE_GROUPS * MOE_EXPERTS

LANES = 128
ROW_TILE = 256
DMA_UNROLL = 8
ATT_TILE_A = 512
ATT_TILE_CQ, ATT_TILE_CK = 512, 256
PAGE_GROUP = 16
LOG2E = math.log2(math.e)
STICK_DEAD = -104.0 * LOG2E
VMEM_LIMIT = 56 * 1024 * 1024


def _params(sem, vmem=VMEM_LIMIT):
    return pltpu.CompilerParams(dimension_semantics=sem, vmem_limit_bytes=vmem)


def _dot(a, b):
    return jnp.dot(a, b, preferred_element_type=F32)


def _dot_nt(a, b):
    return lax.dot_general(a, b, (((1,), (1,)), ((), ())), preferred_element_type=F32)


def _dot_tn(a, b):
    return lax.dot_general(a, b, (((0,), (0,)), ((), ())), preferred_element_type=F32)


def _rms(x):
    return x * lax.rsqrt(jnp.mean(x * x, axis=-1, keepdims=True) + EPS)


def _norm_proj_kernel(x_ref, g_ref, w_ref, o_ref, *extra_refs, head_cols, t_cols, t_tiles):
    xn = _rms(x_ref[...]) * g_ref[...]
    o = _dot(xn.astype(BF16), w_ref[...])
    o_ref[...] = o
    head_refs, t_refs = extra_refs[:len(head_cols)], extra_refs[len(head_cols):]
    for ref, (col, heads) in zip(head_refs, head_cols):
        for hh in range(heads):
            ref[pl.ds(hh, ROW_TILE, stride=heads), :] = o[:, col + hh * LANES:col + (hh + 1) * LANES]

    @pl.when(pl.program_id(0) < t_tiles)
    def _():
        for ref, (col, width) in zip(t_refs, t_cols):
            ref[0] = o[:, col:col + width].T


def _norm_proj(x, g, w, head_cols=(), t_cols=(), t_rows=0, t_len=ROW_TILE):
    n, d = x.shape
    nout = w.shape[1]
    out_specs = [pl.BlockSpec((ROW_TILE, nout), lambda i: (i, 0))]
    out_shape = [jax.ShapeDtypeStruct((n, nout), F32)]
    for _, heads in head_cols:
        out_specs.append(pl.BlockSpec((ROW_TILE * heads, LANES), lambda i: (i, 0)))
        out_shape.append(jax.ShapeDtypeStruct((n * heads, LANES), F32))
    t_tiles = t_rows // ROW_TILE
    per_seq = t_len // ROW_TILE

    def t_index(i):
        j = jnp.minimum(i, t_tiles - 1)
        return j // per_seq, 0, j % per_seq

    for _, width in t_cols:
        out_specs.append(pl.BlockSpec((1, width, ROW_TILE), t_index))
        out_shape.append(jax.ShapeDtypeStruct((t_rows // t_len, width, t_len), F32))
    return pl.pallas_call(
        functools.partial(_norm_proj_kernel, head_cols=tuple(head_cols), t_cols=tuple(t_cols), t_tiles=t_tiles),
        grid=(n // ROW_TILE,),
        in_specs=[
            pl.BlockSpec((ROW_TILE, d), lambda i: (i, 0)),
            pl.BlockSpec((1, d), lambda i: (0, 0)),
            pl.BlockSpec((d, nout), lambda i: (0, 0)),
        ],
        out_specs=out_specs,
        out_shape=out_shape,
        compiler_params=_params(("arbitrary",)),
        name="norm_proj",
    )(x, g.reshape(1, d), w)


def _route(logits):
    lane = lax.broadcasted_iota(jnp.int32, logits.shape, 1).astype(F32)
    neg = jnp.float32(-jnp.inf)
    big = jnp.float32(LANES)
    gl = jnp.where(lane < MOE_GROUPS, logits, neg)
    gmax = jnp.max(gl, axis=-1, keepdims=True)
    g_idx = jnp.min(jnp.where(gl == gmax, lane, big), axis=-1, keepdims=True)
    g_w = 1.0 / jnp.sum(jnp.exp(gl - gmax), axis=-1, keepdims=True)
    lo = MOE_GROUPS + g_idx * MOE_EXPERTS
    el = jnp.where((lane >= lo) & (lane < lo + MOE_EXPERTS), logits, neg)
    v1 = jnp.max(el, axis=-1, keepdims=True)
    i1 = jnp.min(jnp.where(el == v1, lane, big), axis=-1, keepdims=True)
    el2 = jnp.where(lane == i1, neg, el)
    v2 = jnp.max(el2, axis=-1, keepdims=True)
    i2 = jnp.min(jnp.where(el2 == v2, lane, big), axis=-1, keepdims=True)
    e2 = jnp.exp(v2 - v1)
    w1 = g_w / (1.0 + e2)
    w2 = g_w * e2 / (1.0 + e2)
    ids = jnp.where(lane == 0, i1 - MOE_GROUPS, jnp.where(lane == 1, i2 - MOE_GROUPS, 0.0))
    wts = jnp.where(lane == 0, w1, jnp.where(lane == 1, w2, 0.0))
    return ids.astype(jnp.int32), wts


def _out_route_kernel(*refs, n_a, p_tiles):
    a_refs = refs[:2 * n_a]
    w_refs = refs[2 * n_a:3 * n_a]
    h_ref, g_ref, wrh_ref, wrl_ref, br_ref, ho_ref, xn_ref, ids_ref, wts_ref = refs[3 * n_a:]
    is_prompt = pl.program_id(0) < p_tiles
    h = h_ref[...]
    for j, w_ref in enumerate(w_refs):
        a = jnp.where(is_prompt, a_refs[2 * j][...], a_refs[2 * j + 1][...])
        h = h + _dot(a.astype(BF16), w_ref[...])
    ho_ref[...] = h
    xn = _rms(h) * g_ref[...]
    xn_ref[...] = xn
    xh, xl = _split_bf16(xn)
    logits = _dot(xh, wrh_ref[...]) + _dot(xl, wrh_ref[...]) + _dot(xh, wrl_ref[...]) + br_ref[...]
    ids, wts = _route(logits)
    ids_ref[...] = ids
    wts_ref[...] = wts


def _out_route(a_list, w_list, h, g, wr, br):
    n, d = h.shape
    n_a = len(a_list)
    p_tiles = a_list[0][0].shape[0] // ROW_TILE
    row = lambda i: (i, 0)
    const = lambda i: (0, 0)
    in_specs = []
    for a_p, a_s in a_list:
        in_specs.append(pl.BlockSpec((ROW_TILE, a_p.shape[1]), lambda i: (jnp.minimum(i, p_tiles - 1), 0)))
        in_specs.append(pl.BlockSpec((ROW_TILE, a_s.shape[1]), lambda i: (jnp.maximum(i - p_tiles, 0), 0)))
    in_specs += [pl.BlockSpec(w.shape, const) for w in w_list]
    in_specs += [pl.BlockSpec((ROW_TILE, d), row), pl.BlockSpec((1, d), const),
                 pl.BlockSpec((d, LANES), const), pl.BlockSpec((d, LANES), const), pl.BlockSpec((1, LANES), const)]
    wr_hi = wr.astype(BF16)
    wr_lo = (wr - wr_hi.astype(F32)).astype(BF16)
    return pl.pallas_call(
        functools.partial(_out_route_kernel, n_a=n_a, p_tiles=p_tiles),
        grid=(n // ROW_TILE,),
        in_specs=in_specs,
        out_specs=[pl.BlockSpec((ROW_TILE, d), row), pl.BlockSpec((ROW_TILE, d), row),
                   pl.BlockSpec((ROW_TILE, LANES), row), pl.BlockSpec((ROW_TILE, LANES), row)],
        out_shape=[jax.ShapeDtypeStruct((n, d), F32), jax.ShapeDtypeStruct((n, d), F32),
                   jax.ShapeDtypeStruct((n, LANES), jnp.int32), jax.ShapeDtypeStruct((n, LANES), F32)],
        compiler_params=_params(("parallel",)),
        name="out_route",
    )(*[a for pair in a_list for a in pair], *w_list, h, g.reshape(1, d), wr_hi, wr_lo, br)


def _moe_dispatch_kernel(pos_ref, last_ref, nv_ref, x_ref, xs_hbm, zero_scr, sem):
    i = pl.program_id(0)
    n_tiles = xs_hbm.shape[0] // ROW_TILE

    def zero_tile(t):
        start = pl.multiple_of(t * ROW_TILE, ROW_TILE)
        return pltpu.make_async_copy(zero_scr, xs_hbm.at[pl.ds(start, ROW_TILE), :], sem)

    @pl.when(i == 0)
    def _():
        zero_scr[...] = jnp.zeros(zero_scr.shape, F32)
        for e in range(N_EXPERTS):
            @pl.when(last_ref[e] >= 0)
            def _():
                zero_tile(last_ref[e]).start()

        def fill(t, c):
            zero_tile(t).start()
            return c

        lax.fori_loop(nv_ref[0], n_tiles, fill, 0)
        for e in range(N_EXPERTS):
            @pl.when(last_ref[e] >= 0)
            def _():
                zero_tile(0).wait()

        def drain(t, c):
            zero_tile(0).wait()
            return c

        lax.fori_loop(nv_ref[0], n_tiles, drain, 0)

    def issue(r, c):
        for k in range(MOE_TOPK):
            dst = pos_ref[(i * ROW_TILE + r) * MOE_TOPK + k]
            pltpu.make_async_copy(x_ref.at[pl.ds(r, 1), :], xs_hbm.at[pl.ds(dst, 1), :], sem).start()
        return c

    lax.fori_loop(0, ROW_TILE, issue, 0, unroll=DMA_UNROLL)
    for k in range(MOE_TOPK):
        pltpu.make_async_copy(x_ref, xs_hbm.at[pl.ds(0, ROW_TILE), :], sem).wait()


def _moe_dispatch(xn, pos, last_tile, n_valid, n_rows):
    n, d = xn.shape
    grid_spec = pltpu.PrefetchScalarGridSpec(
        num_scalar_prefetch=3,
        grid=(n // ROW_TILE,),
        in_specs=[pl.BlockSpec((ROW_TILE, d), lambda i, pos, last, nv: (i, 0))],
        out_specs=pl.BlockSpec(memory_space=pl.ANY),
        scratch_shapes=[pltpu.VMEM((ROW_TILE, d), F32), pltpu.SemaphoreType.DMA(())],
    )
    return pl.pallas_call(
        _moe_dispatch_kernel,
        grid_spec=grid_spec,
        out_shape=jax.ShapeDtypeStruct((n_rows, d), F32),
        compiler_params=_params(("arbitrary",)),
        name="moe_dispatch",
    )(pos, last_tile, n_valid, xn)


def _moe_ffn_kernel(te_ref, nv_ref, x_ref, wg_ref, wu_ref, wd_ref, o_ref):
    t = pl.program_id(0)

    @pl.when(t < nv_ref[0])
    def _():
        x = x_ref[...].astype(BF16)
        hid = jax.nn.silu(_dot(x, wg_ref[0].astype(BF16))) * _dot(x, wu_ref[0].astype(BF16))
        o_ref[...] = _dot(hid.astype(BF16), wd_ref[0].astype(BF16))

    @pl.when(t >= nv_ref[0])
    def _():
        o_ref[...] = jnp.zeros_like(o_ref)


def _moe_ffn(x_sorted, tile_expert, n_valid, w_gate, w_up, w_down):
    p, d = x_sorted.shape
    f = w_gate.shape[2]
    used = lambda t, te, nv: (jnp.maximum(jnp.minimum(t, nv[0] - 1), 0), 0)
    grid_spec = pltpu.PrefetchScalarGridSpec(
        num_scalar_prefetch=2,
        grid=(p // ROW_TILE,),
        in_specs=[
            pl.BlockSpec((ROW_TILE, d), used),
            pl.BlockSpec((1, d, f), lambda t, te, nv: (te[t], 0, 0)),
            pl.BlockSpec((1, d, f), lambda t, te, nv: (te[t], 0, 0)),
            pl.BlockSpec((1, f, d), lambda t, te, nv: (te[t], 0, 0)),
        ],
        out_specs=pl.BlockSpec((ROW_TILE, d), lambda t, te, nv: (t, 0)),
    )
    return pl.pallas_call(
        _moe_ffn_kernel,
        grid_spec=grid_spec,
        out_shape=jax.ShapeDtypeStruct((p, d), F32),
        compiler_params=_params(("arbitrary",)),
        name="moe_ffn",
    )(tile_expert, n_valid, x_sorted, w_gate, w_up, w_down)


def _moe_combine_kernel(pos_ref, h_ref, w_ref, *refs, final_norm, p_tiles):
    g_ref = refs[0] if final_norm else None
    y_hbm = refs[1 if final_norm else 0]
    rbuf, sem = refs[-2:]
    o_refs = refs[(2 if final_norm else 1):-2]
    i = pl.program_id(0)

    def gather(tile, slot):
        def issue(r, c):
            for k in range(MOE_TOPK):
                src = pos_ref[(tile * ROW_TILE + r) * MOE_TOPK + k]
                pltpu.make_async_copy(y_hbm.at[pl.ds(src, 1), :], rbuf.at[slot, k, pl.ds(r, 1), :],
                                      sem.at[slot]).start()
            return c

        lax.fori_loop(0, ROW_TILE, issue, 0, unroll=DMA_UNROLL)

    @pl.when(i == 0)
    def _():
        gather(0, 0)

    @pl.when(i + 1 < pl.num_programs(0))
    def _():
        gather(i + 1, (i + 1) % 2)

    slot = i % 2
    for k in range(MOE_TOPK):
        pltpu.make_async_copy(y_hbm.at[pl.ds(0, ROW_TILE), :], rbuf.at[slot, k], sem.at[slot]).wait()
    w = w_ref[...]
    acc = h_ref[...]
    for k in range(MOE_TOPK):
        acc = acc + w[:, k:k + 1] * rbuf[slot, k]
    if not final_norm:
        o_refs[0][...] = acc
    else:
        acc = _rms(acc) * g_ref[...]

        @pl.when(i < p_tiles)
        def _():
            o_refs[0][...] = acc

        @pl.when(i >= p_tiles)
        def _():
            o_refs[1][...] = acc


def _moe_combine(h, wts, y_sorted, pos, gain=None, n_prompt=0):
    n, d = h.shape
    row = lambda i, pos: (i, 0)
    in_specs = [pl.BlockSpec((ROW_TILE, d), row), pl.BlockSpec((ROW_TILE, LANES), row)]
    args = [h, wts]
    p_tiles = n_prompt // ROW_TILE
    if gain is None:
        out_specs = [pl.BlockSpec((ROW_TILE, d), row)]
        out_shape = [jax.ShapeDtypeStruct((n, d), F32)]
    else:
        in_specs.append(pl.BlockSpec((1, d), lambda i, pos: (0, 0)))
        args.append(gain.reshape(1, d))
        out_specs = [pl.BlockSpec((ROW_TILE, d), lambda i, pos: (jnp.minimum(i, p_tiles - 1), 0)),
                     pl.BlockSpec((ROW_TILE, d), lambda i, pos: (jnp.maximum(i - p_tiles, 0), 0))]
        out_shape = [jax.ShapeDtypeStruct((n_prompt, d), F32), jax.ShapeDtypeStruct((n - n_prompt, d), F32)]
    grid_spec = pltpu.PrefetchScalarGridSpec(
        num_scalar_prefetch=1,
        grid=(n // ROW_TILE,),
        in_specs=in_specs + [pl.BlockSpec(memory_space=pl.ANY)],
        out_specs=out_specs,
        scratch_shapes=[pltpu.VMEM((2, MOE_TOPK, ROW_TILE, d), F32), pltpu.SemaphoreType.DMA((2,))],
    )
    return pl.pallas_call(
        functools.partial(_moe_combine_kernel, final_norm=gain is not None, p_tiles=p_tiles),
        grid_spec=grid_spec,
        out_shape=out_shape,
        compiler_params=_params(("arbitrary",)),
        name="moe_combine",
    )(pos, *args, y_sorted)


def _dispatch_plan(ids):
    n = ids.shape[0]
    a = n * MOE_TOPK
    n_tiles = a // ROW_TILE + N_EXPERTS
    e_ids = jnp.arange(N_EXPERTS, dtype=jnp.int32)
    onehot = (ids.reshape(a, 1) == e_ids[None, :]).astype(jnp.int32)
    seen = jnp.cumsum(onehot, axis=0)
    counts = seen[-1]
    tiles_per = (counts + ROW_TILE - 1) // ROW_TILE
    tile_end = jnp.cumsum(tiles_per)
    tile_start = tile_end - tiles_per
    pos = jnp.sum(onehot * (seen - 1 + (tile_start * ROW_TILE)[None, :]), axis=1).astype(jnp.int32)
    n_valid = tile_end[-1].astype(jnp.int32)
    t_idx = jnp.arange(n_tiles, dtype=jnp.int32)
    te = jnp.minimum(jnp.sum((tile_end[None, :] <= t_idx[:, None]).astype(jnp.int32), axis=1), N_EXPERTS - 1)
    last = jnp.max(jnp.where(tiles_per > 0, e_ids, 0))
    tile_expert = jnp.where(t_idx < n_valid, te, last).astype(jnp.int32)
    last_tile = jnp.where(tiles_per > 0, tile_end - 1, -1).astype(jnp.int32)
    return tile_expert, n_valid.reshape(1), pos, last_tile, n_tiles * ROW_TILE


def _moe(h, xn, ids, wts, w_gate, w_up, w_down, final_gain=None, n_prompt=0):
    tile_expert, n_valid, pos, last_tile, n_rows = _dispatch_plan(ids[:, :MOE_TOPK])
    x_sorted = _moe_dispatch(xn, pos, last_tile, n_valid, n_rows)
    y_sorted = _moe_ffn(x_sorted, tile_expert, n_valid, w_gate, w_up, w_down)
    return _moe_combine(h, wts, y_sorted, pos, final_gain, n_prompt)


def _lambda_from(lam_ref):
    lp = lam_ref[...]
    s1 = jnp.sum(lp[0:1, :] * lp[1:2, :], axis=-1, keepdims=True)
    s2 = jnp.sum(lp[2:3, :] * lp[3:4, :], axis=-1, keepdims=True)
    return jnp.exp(s1) - jnp.exp(s2) + LAMBDA_INIT


def _transpose_tiles(src_ref, dst_ref, tile):
    def body(j, c):
        start = pl.multiple_of(j * tile, tile)
        dst_ref[j] = src_ref[pl.ds(start, tile), :].T.astype(BF16)
        return c

    lax.fori_loop(0, dst_ref.shape[0], body, 0)


def _diff_prompt_kernel(slopes_ref, q_ref, k_ref, v_ref, lam_ref, g_ref, o_ref, vt_scr, s_scr, m_scr, l_scr, acc_scr,
                        *, tq, tk):
    h = pl.program_id(1)
    qi = pl.program_id(2)

    @pl.when(qi == 0)
    def _():
        _transpose_tiles(v_ref, vt_scr, tk)

    slope = slopes_ref[h] * LOG2E
    q = q_ref[...] * (A_DH ** -0.5 * LOG2E)
    lane = lax.broadcasted_iota(jnp.int32, q.shape, 1)
    qcat = jnp.concatenate([jnp.where(lane < A_DH, q, 0.0), jnp.where(lane >= A_DH, q, 0.0)], axis=0).astype(BF16)
    key = lax.broadcasted_iota(jnp.int32, (tk, 2 * tq), 0)
    qry = lax.broadcasted_iota(jnp.int32, (tk, 2 * tq), 1) % tq
    bias0 = slope * key.astype(F32)
    m_scr[...] = jnp.full(m_scr.shape, -jnp.inf, F32)
    l_scr[...] = jnp.zeros(l_scr.shape, F32)
    acc_scr[...] = jnp.zeros(acc_scr.shape, F32)

    def scores(j):
        start = pl.multiple_of(j * tk, tk)
        return _dot_nt(k_ref[pl.ds(start, tk), :].astype(BF16), qcat)

    def update(slot, j, mask):
        cb = slope * (j * tk).astype(F32)
        t = s_scr[slot] + bias0
        if mask is not None:
            t = jnp.where(mask, t, -jnp.inf)
        m_old = m_scr[...]
        m_new = jnp.maximum(m_old, jnp.max(t, axis=0, keepdims=True) + cb)
        p = jnp.exp2(t + (cb - m_new))
        alpha = jnp.exp2(m_old - m_new)
        l_scr[...] = alpha * l_scr[...] + jnp.sum(p, axis=0, keepdims=True)
        acc_scr[...] = alpha * acc_scr[...] + _dot(vt_scr[j], p.astype(BF16))
        m_scr[...] = m_new

    s_scr[0] = scores(0)

    def body(i, carry):
        j = 2 * i
        s_scr[1] = scores(j + 1)
        update(0, j, None)
        s_scr[0] = scores(j + 2)
        update(1, j + 1, None)
        return carry

    lax.fori_loop(0, qi // 2, body, 0)
    diag = key <= qry

    @pl.when(qi % 2 == 1)
    def _():
        s_scr[1] = scores(qi)
        update(0, qi - 1, None)
        update(1, qi, diag)

    @pl.when(qi % 2 == 0)
    def _():
        update(0, qi, diag)

    lam = _lambda_from(lam_ref)
    o = acc_scr[...] / l_scr[...]
    o = (o[:, :tq] - lam * o[:, tq:]).T
    o_ref[...] = _rms(o) * g_ref[...] * (1.0 - LAMBDA_INIT)


def _diff_prompt(proj, slopes, lam_pack, subln, n_batch, t_len):
    tq = tk = min(ATT_TILE_A, t_len)
    nq = t_len // tq
    qcol, kcol, vcol = 0, A_HEADS, 2 * A_HEADS
    grid_spec = pltpu.PrefetchScalarGridSpec(
        num_scalar_prefetch=1,
        grid=(n_batch, A_HEADS, nq),
        in_specs=[
            pl.BlockSpec((tq, LANES), lambda b, h, i, s: (b * nq + i, qcol + h)),
            pl.BlockSpec((t_len, LANES), lambda b, h, i, s: (b, kcol + h)),
            pl.BlockSpec((t_len, LANES), lambda b, h, i, s: (b, vcol + h)),
            pl.BlockSpec((8, LANES), lambda b, h, i, s: (0, 0)),
            pl.BlockSpec((1, LANES), lambda b, h, i, s: (0, 0)),
        ],
        out_specs=pl.BlockSpec((tq, LANES), lambda b, h, i, s: (b * nq + i, h)),
        scratch_shapes=[pltpu.VMEM((t_len // tk, LANES, tk), BF16), pltpu.VMEM((2, tk, 2 * tq), F32),
                        pltpu.VMEM((1, 2 * tq), F32), pltpu.VMEM((1, 2 * tq), F32), pltpu.VMEM((LANES, 2 * tq), F32)],
    )
    return pl.pallas_call(
        functools.partial(_diff_prompt_kernel, tq=tq, tk=tk),
        grid_spec=grid_spec,
        out_shape=jax.ShapeDtypeStruct((n_batch * t_len, A_HEADS * A_DV), F32),
        compiler_params=_params(("parallel", "parallel", "arbitrary")),
        name="diff_prompt",
    )(slopes, proj, proj, proj, lam_pack, subln.reshape(1, LANES))


def _diff_sample_kernel(pt_ref, slopes_ref, q_ref, kn_ref, vn_ref, *refs, ts, page, n_pages, group):
    kp_refs, vp_refs = refs[:group], refs[group:2 * group]
    lam_ref, g_ref, o_ref, m_scr, l_scr, acc_scr = refs[2 * group:]
    p = pl.program_id(1)
    hrows = 2 * ts
    rows = A_HEADS * hrows
    past = n_pages * page
    q = q_ref[...] * (A_DH ** -0.5 * LOG2E)
    lane = lax.broadcasted_iota(jnp.int32, (ts, LANES), 1)
    qh = []
    for hh in range(A_HEADS):
        qq = q[:, hh * LANES:(hh + 1) * LANES]
        qh.append(jnp.concatenate([jnp.where(lane < A_DH, qq, 0.0), jnp.where(lane >= A_DH, qq, 0.0)], axis=0).astype(BF16))
    rid = lax.broadcasted_iota(jnp.int32, (rows, 1), 0)
    rt = rid % ts
    slope_rows = jnp.zeros((rows, 1), F32)
    for hh in range(A_HEADS):
        slope_rows = jnp.where(rid // hrows == hh, slopes_ref[hh] * LOG2E, slope_rows)

    def update(k_of, v_of, kpos, new_tokens):
        s = jnp.concatenate([jnp.concatenate([_dot_nt(qh[hh], k.astype(BF16)) for k in k_of(hh)], axis=1)
                             for hh in range(A_HEADS)], axis=0)
        t = s + slope_rows * kpos.astype(F32)
        if new_tokens:
            t = jnp.where(kpos - past <= rt, t, -jnp.inf)
        m_old = m_scr[...]
        m_new = jnp.maximum(m_old, jnp.max(t, axis=-1, keepdims=True))
        pr = jnp.exp2(t - m_new)
        alpha = jnp.exp2(m_old - m_new)
        l_scr[...] = alpha * l_scr[...] + jnp.sum(pr, axis=-1, keepdims=True)
        for hh in range(A_HEADS):
            sl = slice(hh * hrows, (hh + 1) * hrows)
            pv, off = 0.0, 0
            for v in v_of(hh):
                pv = pv + _dot(pr[sl, off:off + v.shape[0]].astype(BF16), v.astype(BF16))
                off += v.shape[0]
            acc_scr[hh] = alpha[sl] * acc_scr[hh] + pv
        m_scr[...] = m_new

    @pl.when(p == 0)
    def _():
        m_scr[...] = jnp.full(m_scr.shape, -jnp.inf, F32)
        l_scr[...] = jnp.zeros(l_scr.shape, F32)
        acc_scr[...] = jnp.zeros(acc_scr.shape, F32)
        kpos = past + lax.broadcasted_iota(jnp.int32, (1, ts), 1)
        update(lambda hh: [kn_ref[:, hh * LANES:(hh + 1) * LANES]], lambda hh: [vn_ref[:, hh * A_DV:(hh + 1) * A_DV]],
               kpos, True)

    kpos = p * (group * page) + lax.broadcasted_iota(jnp.int32, (1, group * page), 1)
    update(lambda hh: [r[0, pl.ds(hh, page, stride=A_HEADS), :] for r in kp_refs],
           lambda hh: [r[0, pl.ds(hh, page, stride=A_HEADS), :] for r in vp_refs], kpos, False)

    @pl.when(p == n_pages // group - 1)
    def _():
        lam = _lambda_from(lam_ref)
        g = g_ref[...]
        inv_l = 1.0 / l_scr[...]
        outs = []
        for hh in range(A_HEADS):
            o = acc_scr[hh] * inv_l[hh * hrows:(hh + 1) * hrows]
            o = o[:ts] - lam * o[ts:]
            outs.append(_rms(o) * g * (1.0 - LAMBDA_INIT))
        o_ref[...] = jnp.concatenate(outs, axis=-1)


def _diff_sample(proj, row0, cache_k, cache_v, page_table, slopes, lam_pack, subln, n_seq, ts):
    n_pages = page_table.shape[1]
    page = cache_k.shape[1]
    width = A_HEADS * 2 * A_DH
    vwidth = A_HEADS * A_DV
    rb = row0 // ts
    rows = 2 * A_HEADS * ts
    qblk, kblk, vblk = 0, 1, 2
    group = math.gcd(n_pages, PAGE_GROUP)

    def page_spec(i, w):
        return pl.BlockSpec((1, page * A_HEADS, w), lambda s, p, pt, sl: (pt[s, p * group + i], 0, 0))

    grid_spec = pltpu.PrefetchScalarGridSpec(
        num_scalar_prefetch=2,
        grid=(n_seq, n_pages // group),
        in_specs=[
            pl.BlockSpec((ts, width), lambda s, p, pt, sl: (rb + s, qblk)),
            pl.BlockSpec((ts, width), lambda s, p, pt, sl: (rb + s, kblk)),
            pl.BlockSpec((ts, vwidth), lambda s, p, pt, sl: (rb + s, vblk)),
            *[page_spec(i, 2 * A_DH) for i in range(group)],
            *[page_spec(i, A_DV) for i in range(group)],
            pl.BlockSpec((8, LANES), lambda s, p, pt, sl: (0, 0)),
            pl.BlockSpec((1, LANES), lambda s, p, pt, sl: (0, 0)),
        ],
        out_specs=pl.BlockSpec((ts, vwidth), lambda s, p, pt, sl: (s, 0)),
        scratch_shapes=[pltpu.VMEM((rows, 1), F32), pltpu.VMEM((rows, 1), F32),
                        pltpu.VMEM((A_HEADS, 2 * ts, A_DV), F32)],
    )
    k_pages = cache_k.reshape(-1, page * A_HEADS, 2 * A_DH)
    v_pages = cache_v.reshape(-1, page * A_HEADS, A_DV)
    return pl.pallas_call(
        functools.partial(_diff_sample_kernel, ts=ts, page=page, n_pages=n_pages, group=group),
        grid_spec=grid_spec,
        out_shape=jax.ShapeDtypeStruct((n_seq * ts, vwidth), F32),
        compiler_params=_params(("parallel", "arbitrary")),
        name="diff_sample",
    )(page_table, slopes, proj, proj, proj, *([k_pages] * group), *([v_pages] * group), lam_pack,
      subln.reshape(1, LANES))


def _retention_kernel(q_ref, k_ref, v_ref, g_ref, cos_ref, sin_ref, s0_ref, o_ref, s_out_ref, s_scr, *, chunk):
    c = pl.program_id(1)

    @pl.when(c == 0)
    def _():
        s_scr[...] = s0_ref[0]

    cos = cos_ref[...]
    sin = sin_ref[...]
    width = B_HEADS * B_DK
    lane = lax.broadcasted_iota(jnp.int32, (chunk, width), 1)
    first_half = (lane % B_DK) < (B_DK // 2)

    def rope(x):
        partner = jnp.where(first_half, pltpu.roll(x, width - B_DK // 2, 1), pltpu.roll(x, B_DK // 2, 1))
        return x * cos + partner * sin

    q = rope(q_ref[...])
    k = rope(k_ref[...]) * (B_DK ** -0.5)
    v = v_ref[...]
    g = g_ref[...]
    n_col = lax.broadcasted_iota(jnp.int32, (chunk, 1), 0).astype(F32)
    ri = lax.broadcasted_iota(jnp.int32, (chunk, chunk), 0)
    ci = lax.broadcasted_iota(jnp.int32, (chunk, chunk), 1)
    dist = (ri - ci).astype(F32)
    outs = []
    for hh in range(B_HEADS):
        log_g = math.log1p(-(2.0 ** (-5.0 - hh)))
        decay = jnp.where(ri >= ci, jnp.exp(log_g * jnp.maximum(dist, 0.0)), 0.0)
        qh = q[:, hh * B_DK:(hh + 1) * B_DK]
        kh = k[:, hh * B_DK:(hh + 1) * B_DK]
        vh = v[:, hh * B_DV:(hh + 1) * B_DV].astype(BF16)
        s_h = s_scr[hh]
        inner = _dot_nt(qh.astype(BF16), kh.astype(BF16)) * decay
        q_dec = qh * jnp.exp(log_g * (n_col + 1.0))
        o = _dot(inner.astype(BF16), vh) + _dot(q_dec.astype(BF16), s_h.astype(BF16))
        k_dec = kh * jnp.exp(log_g * (chunk - 1.0 - n_col))
        s_scr[hh] = math.exp(log_g * chunk) * s_h + _dot_tn(k_dec.astype(BF16), vh)
        gh = g[:, hh * B_DV:(hh + 1) * B_DV]
        outs.append(jax.nn.silu(gh) * _rms(o))
    o_ref[...] = jnp.concatenate(outs, axis=-1)

    @pl.when(c == pl.num_programs(1) - 1)
    def _():
        s_out_ref[0] = s_scr[...]


def _retention(proj, row0, cos_t, sin_t, s0, n_seq, t_len):
    chunk = CHUNK if (t_len > CHUNK and t_len % CHUNK == 0) else t_len
    nc = t_len // chunk
    rb = row0 // chunk
    qw, vw = B_HEADS * B_DK, B_HEADS * B_DV
    qcol, kcol, vcol, gcol = 6, 7, 4, 5
    row = lambda s, c: (rb + s * nc + c, 0)
    return pl.pallas_call(
        functools.partial(_retention_kernel, chunk=chunk),
        grid=(n_seq, nc),
        in_specs=[
            pl.BlockSpec((chunk, qw), lambda s, c: (rb + s * nc + c, qcol)),
            pl.BlockSpec((chunk, qw), lambda s, c: (rb + s * nc + c, kcol)),
            pl.BlockSpec((chunk, vw), lambda s, c: (rb + s * nc + c, vcol)),
            pl.BlockSpec((chunk, vw), lambda s, c: (rb + s * nc + c, gcol)),
            pl.BlockSpec((chunk, qw), lambda s, c: (c, 0)),
            pl.BlockSpec((chunk, qw), lambda s, c: (c, 0)),
            pl.BlockSpec((1, B_HEADS, B_DK, B_DV), lambda s, c: (s, 0, 0, 0)),
        ],
        out_specs=[pl.BlockSpec((chunk, vw), lambda s, c: (s * nc + c, 0)),
                   pl.BlockSpec((1, B_HEADS, B_DK, B_DV), lambda s, c: (s, 0, 0, 0))],
        out_shape=[jax.ShapeDtypeStruct((n_seq * t_len, vw), F32),
                   jax.ShapeDtypeStruct((n_seq, B_HEADS, B_DK, B_DV), F32)],
        scratch_shapes=[pltpu.VMEM((B_HEADS, B_DK, B_DV), F32)],
        compiler_params=_params(("parallel", "arbitrary")),
        name="retention",
    )(proj, proj, proj, proj, cos_t, sin_t, s0)


def _rope_tables(pos):
    half = B_DK // 2
    inv = ROPE_BASE ** (-jnp.arange(half, dtype=F32) / half)
    ang = pos.astype(F32)[:, None] * inv[None, :]
    cos, sin = jnp.cos(ang), jnp.sin(ang)
    cos_t = jnp.tile(jnp.concatenate([cos, cos], axis=-1), (1, B_HEADS))
    sin_t = jnp.tile(jnp.concatenate([-sin, sin], axis=-1), (1, B_HEADS))
    return cos_t, sin_t


def _softplus2(z2):
    return jnp.maximum(z2, 0.0) + jnp.log2(1.0 + jnp.exp2(-jnp.abs(z2)))


def _split_bf16(x):
    hi = x.astype(BF16)
    return hi, (x - hi.astype(F32)).astype(BF16)


def _stick_weights_t(zt, carry, newer_mat, mask):
    sp = _softplus2(zt)
    lk = -sp
    if mask is not None:
        lk = jnp.where(mask, lk, 0.0)
    hi, lo = _split_bf16(lk)
    newer = _dot(newer_mat, hi) + _dot(newer_mat, lo)
    a = jnp.exp2((zt - sp) + newer + carry)
    if mask is not None:
        a = jnp.where(mask, a, 0.0)
    return a, carry + jnp.sum(lk, axis=0, keepdims=True)


def _stick_prompt_kernel(q_ref, k_ref, v_ref, o_ref, vt_scr, carry_scr, acc_scr, *, tq, tk):
    qi = pl.program_id(2)

    @pl.when(qi == 0)
    def _():
        _transpose_tiles(v_ref, vt_scr, tk)

    heads = LANES // C_DH
    width = heads * tq
    q = q_ref[...] * (C_DH ** -0.5 * LOG2E)
    lane = lax.broadcasted_iota(jnp.int32, q.shape, 1)
    qcat = jnp.concatenate([jnp.where(lane // C_DH == hh, q, 0.0) for hh in range(heads)], axis=0).astype(BF16)
    key = lax.broadcasted_iota(jnp.int32, (tk, width), 0)
    qry = lax.broadcasted_iota(jnp.int32, (tk, width), 1) % tq
    rj = lax.broadcasted_iota(jnp.int32, (tk, tk), 0)
    cs = lax.broadcasted_iota(jnp.int32, (tk, tk), 1)
    newer_mat = jnp.where(cs > rj, 1.0, 0.0).astype(BF16)
    sub = tq // tk

    def block(j, mask, carry):
        start = pl.multiple_of(j * tk, tk)
        zt = _dot_nt(k_ref[pl.ds(start, tk), :].astype(BF16), qcat)
        a, carry = _stick_weights_t(zt, carry, newer_mat, mask)
        return _dot(vt_scr[j], a.astype(BF16)), carry

    carry = jnp.zeros((1, width), F32)
    for jj in reversed(range(sub)):
        av, carry = block(qi * sub + jj, (jj * tk + key) < qry, carry)
        if jj == sub - 1:
            acc_scr[...] = av
        else:
            acc_scr[...] += av
    carry_scr[...] = carry

    def cond(state):
        j, cmax = state
        return (j >= 0) & (cmax > STICK_DEAD)

    def body(state):
        j, _ = state
        av, carry = block(j, None, carry_scr[...])
        acc_scr[...] += av
        carry_scr[...] = carry
        return j - 1, jnp.max(carry)

    lax.while_loop(cond, body, (qi * sub - 1, jnp.max(carry)))
    acc = acc_scr[...]
    o_t = jnp.concatenate([acc[hh * C_DH:(hh + 1) * C_DH, hh * tq:(hh + 1) * tq] for hh in range(heads)], axis=0)
    o_ref[...] = o_t.T


def _stick_prompt(proj, n_batch, t_len):
    tq = min(ATT_TILE_CQ, t_len)
    tk = min(ATT_TILE_CK, tq)
    nq = t_len // tq
    ng = C_HEADS * C_DH // LANES
    return pl.pallas_call(
        functools.partial(_stick_prompt_kernel, tq=tq, tk=tk),
        grid=(n_batch, ng, nq),
        in_specs=[
            pl.BlockSpec((tq, LANES), lambda b, g, i: (b * nq + i, g)),
            pl.BlockSpec((t_len, LANES), lambda b, g, i: (b, ng + g)),
            pl.BlockSpec((t_len, LANES), lambda b, g, i: (b, 2 * ng + g)),
        ],
        out_specs=pl.BlockSpec((tq, LANES), lambda b, g, i: (b * nq + i, g)),
        out_shape=jax.ShapeDtypeStruct((n_batch * t_len, C_HEADS * C_DH), F32),
        scratch_shapes=[pltpu.VMEM((t_len // tk, LANES, tk), BF16), pltpu.VMEM((1, LANES // C_DH * tq), F32),
                        pltpu.VMEM((LANES, LANES // C_DH * tq), F32)],
        compiler_params=_params(("parallel", "parallel", "arbitrary")),
        name="stick_prompt",
    )(proj, proj, proj)


def _stick_sample_kernel(pt_ref, q_ref, kn_ref, vn_ref, ck_hbm, cv_hbm, o_ref, kbuf, vbuf, ksem, vsem,
                         carry_scr, acc_scr, *, ts, page, n_pages):
    s = pl.program_id(0)
    rows = C_HEADS * ts

    def copies(seq, i, slot):
        pid = pt_ref[seq, n_pages - 1 - i]
        return (pltpu.make_async_copy(ck_hbm.at[pid], kbuf.at[slot], ksem.at[slot]),
                pltpu.make_async_copy(cv_hbm.at[pid], vbuf.at[slot], vsem.at[slot]))

    def start(seq, i, slot):
        for cp in copies(seq, i, slot):
            cp.start()

    def wait(seq, i, slot):
        for cp in copies(seq, i, slot):
            cp.wait()

    @pl.when(s == 0)
    def _():
        start(0, 0, 0)
        start(0, 1, 1)

    q = q_ref[...] * (C_DH ** -0.5 * LOG2E)
    qh = [q[:, hh * C_DH:(hh + 1) * C_DH].astype(BF16) for hh in range(C_HEADS)]
    r = lax.broadcasted_iota(jnp.int32, (page, page), 0)
    c = lax.broadcasted_iota(jnp.int32, (page, page), 1)
    upper = jnp.where(r > c, 1.0, 0.0).astype(BF16)

    def tile(z_of, av_of, carry, mask):
        z = jnp.concatenate([z_of(hh) for hh in range(C_HEADS)], axis=0)
        sp = _softplus2(z)
        lk = -sp
        if mask is not None:
            lk = jnp.where(mask, lk, 0.0)
        hi, lo = _split_bf16(lk)
        newer = _dot(hi, upper) + _dot(lo, upper)
        a = jnp.exp2((z - sp) + newer + carry)
        if mask is not None:
            a = jnp.where(mask, a, 0.0)
        for hh in range(C_HEADS):
            acc_scr[hh] += av_of(hh, a[hh * ts:(hh + 1) * ts].astype(BF16))
        return carry + jnp.sum(lk, axis=-1, keepdims=True)

    acc_scr[...] = jnp.zeros(acc_scr.shape, F32)
    pad = jnp.zeros((page - ts, C_DH), F32)
    rt = lax.broadcasted_iota(jnp.int32, (rows, page), 0) % ts
    kc = lax.broadcasted_iota(jnp.int32, (rows, page), 1)

    def new_rows(ref, hh):
        return jnp.concatenate([ref[:, hh * C_DH:(hh + 1) * C_DH], pad], axis=0).astype(BF16)

    carry = tile(lambda hh: _dot_nt(qh[hh], new_rows(kn_ref, hh)), lambda hh, a: _dot(a, new_rows(vn_ref, hh)),
                 jnp.zeros((rows, 1), F32), kc < rt)

    def page_tile(slot, carry):
        return tile(lambda hh: _dot(qh[hh], kbuf[slot, hh].astype(BF16)),
                    lambda hh, a: _dot_nt(a, vbuf[slot, hh].astype(BF16)), carry, None)

    wait(s, 0, 0)
    carry = page_tile(0, carry)
    cmax = jnp.max(carry)
    carry_scr[...] = carry

    @pl.when((2 < n_pages) & (cmax > STICK_DEAD))
    def _():
        start(s, 2, 0)

    def cond(state):
        i, cmax = state
        return (i < n_pages) & (cmax > STICK_DEAD)

    def body(state):
        i, _ = state
        slot = i % 2
        wait(s, i, slot)
        carry = page_tile(slot, carry_scr[...])
        cmax = jnp.max(carry)
        carry_scr[...] = carry

        @pl.when((i + 2 < n_pages) & (cmax > STICK_DEAD))
        def _():
            start(s, i + 2, slot)

        return i + 1, cmax

    i_end, _ = lax.while_loop(cond, body, (jnp.int32(1), cmax))

    @pl.when(i_end < n_pages)
    def _():
        wait(s, i_end, i_end % 2)

    @pl.when(s + 1 < pl.num_programs(0))
    def _():
        start(s + 1, 0, 0)
        start(s + 1, 1, 1)

    o_ref[0] = acc_scr[...]


def _stick_sample(proj, row0, cache_k, cache_v, page_table, n_seq, ts):
    n_pages = page_table.shape[1]
    page = cache_k.shape[1]
    assert n_pages >= 2
    width = C_HEADS * C_DH
    rb = row0 // ts
    grid_spec = pltpu.PrefetchScalarGridSpec(
        num_scalar_prefetch=1,
        grid=(n_seq,),
        in_specs=[
            pl.BlockSpec((ts, width), lambda s, pt: (rb + s, 0)),
            pl.BlockSpec((ts, width), lambda s, pt: (rb + s, 1)),
            pl.BlockSpec((ts, width), lambda s, pt: (rb + s, 2)),
            pl.BlockSpec(memory_space=pl.ANY),
            pl.BlockSpec(memory_space=pl.ANY),
        ],
        out_specs=pl.BlockSpec((1, C_HEADS, ts, C_DH), lambda s, pt: (s, 0, 0, 0)),
        scratch_shapes=[pltpu.VMEM((2, C_HEADS, C_DH, page), F32), pltpu.VMEM((2, C_HEADS, C_DH, page), F32),
                        pltpu.SemaphoreType.DMA((2,)), pltpu.SemaphoreType.DMA((2,)),
                        pltpu.VMEM((C_HEADS * ts, 1), F32), pltpu.VMEM((C_HEADS, ts, C_DH), F32)],
    )
    out = pl.pallas_call(
        functools.partial(_stick_sample_kernel, ts=ts, page=page, n_pages=n_pages),
        grid_spec=grid_spec,
        out_shape=jax.ShapeDtypeStruct((n_seq, C_HEADS, ts, C_DH), F32),
        compiler_params=_params(("arbitrary",)),
        name="stick_sample",
    )(page_table, proj, proj, proj, jnp.transpose(cache_k, (0, 2, 3, 1)), jnp.transpose(cache_v, (0, 2, 3, 1)))
    return jnp.transpose(out, (0, 2, 1, 3)).reshape(n_seq * ts, width)


def _router_pack(wg, bg, we, be):
    d = wg.shape[0]
    we_flat = jnp.transpose(we, (1, 0, 2)).reshape(d, N_EXPERTS)
    w = jnp.concatenate([wg, we_flat], axis=1)
    b = jnp.concatenate([bg, be.reshape(N_EXPERTS)])
    pad = LANES - w.shape[1]
    return jnp.pad(w, ((0, 0), (0, pad))), jnp.pad(b, (0, pad)).reshape(1, LANES)


def _expert_weights(w_gate, w_up, w_down):
    d, f = w_gate.shape[2], w_gate.shape[3]
    return w_gate.reshape(N_EXPERTS, d, f), w_up.reshape(N_EXPERTS, d, f), w_down.reshape(N_EXPERTS, f, d)


def kernel(x_prompt, x_sample, cache_a_k, cache_a_v, state_b, cache_c_k, cache_c_v, page_table, ln0_mix, w_in0, lam_q1, lam_k1, lam_q2, lam_k2, a_subln, w_out0, ln0_ffn, moe0_wg, moe0_bg, moe0_we, moe0_be, moe0_w_gate, moe0_w_up, moe0_w_down, ln1_mix, w_in1, w_out1, ln1_ffn, moe1_wg, moe1_bg, moe1_we, moe1_be, moe1_w_gate, moe1_w_up, moe1_w_down, ln_f):
    bp, tp, d = x_prompt.shape
    db, ts, _ = x_sample.shape
    n_p, n_s = bp * tp, db * ts
    past = page_table.shape[1] * cache_a_k.shape[1]
    h = jnp.concatenate([x_prompt.reshape(n_p, d), x_sample.reshape(n_s, d)], axis=0)

    slopes = jnp.exp2(-8.0 * jnp.arange(1, A_HEADS + 1, dtype=F32) / A_HEADS)
    lam_pack = jnp.pad(jnp.stack([lam_q1, lam_k1, lam_q2, lam_k2]), ((0, 4), (0, LANES - A_DH)))
    cos_p, sin_p = _rope_tables(jnp.arange(tp, dtype=jnp.int32))
    cos_s, sin_s = _rope_tables(past + jnp.arange(ts, dtype=jnp.int32))

    a_w = A_HEADS * 2 * A_DH
    proj0, a_k_rows, a_v_rows = _norm_proj(h, ln0_mix, w_in0.astype(BF16), ((a_w, A_HEADS), (2 * a_w, A_HEADS)))
    a_o_p = _diff_prompt(proj0, slopes, lam_pack, a_subln, bp, tp)
    a_o_s = _diff_sample(proj0, n_p, cache_a_k, cache_a_v, page_table, slopes, lam_pack, a_subln, db, ts)
    b_o_p, b_state_p = _retention(proj0, 0, cos_p, sin_p, jnp.zeros((bp, B_HEADS, B_DK, B_DV), F32), bp, tp)
    b_o_s, b_state_s = _retention(proj0, n_p, cos_s, sin_s, state_b.astype(F32), db, ts)
    w_out0_b = w_out0.astype(BF16)
    wr0, br0 = _router_pack(moe0_wg, moe0_bg, moe0_we, moe0_be)
    h, xn, ids, wts = _out_route([(a_o_p, a_o_s), (b_o_p, b_o_s)],
                                 [w_out0_b[:A_HEADS * A_DV], w_out0_b[A_HEADS * A_DV:]], h, ln0_ffn, wr0, br0)
    (h,) = _moe(h, xn, ids, wts, *_expert_weights(moe0_w_gate, moe0_w_up, moe0_w_down))

    c_w = C_HEADS * C_DH
    proj1, c_k_t, c_v_t = _norm_proj(h, ln1_mix, w_in1.astype(BF16), t_cols=((c_w, c_w), (2 * c_w, c_w)),
                                     t_rows=n_p, t_len=tp)
    c_o_p = _stick_prompt(proj1, bp, tp)
    c_o_s = _stick_sample(proj1, n_p, cache_c_k, cache_c_v, page_table, db, ts)
    wr1, br1 = _router_pack(moe1_wg, moe1_bg, moe1_we, moe1_be)
    h, xn, ids, wts = _out_route([(c_o_p, c_o_s)], [w_out1.astype(BF16)], h, ln1_ffn, wr1, br1)
    y_p, y_s = _moe(h, xn, ids, wts, *_expert_weights(moe1_w_gate, moe1_w_up, moe1_w_down), final_gain=ln_f,
                    n_prompt=n_p)

    def heads_last(a_t):
        return jnp.transpose(a_t.reshape(bp, C_HEADS, C_DH, tp), (0, 3, 1, 2))

    y_p, y_s = y_p.reshape(bp, tp, d), y_s.reshape(db, ts, d)
    a_k_p, a_k_s = a_k_rows[:n_p * A_HEADS].reshape(bp, tp, A_HEADS, 2 * A_DH), a_k_rows[n_p * A_HEADS:].reshape(db, ts, A_HEADS, 2 * A_DH)
    a_v_p, a_v_s = a_v_rows[:n_p * A_HEADS].reshape(bp, tp, A_HEADS, A_DV), a_v_rows[n_p * A_HEADS:].reshape(db, ts, A_HEADS, A_DV)
    c_k_p, c_v_p = heads_last(c_k_t), heads_last(c_v_t)
    c_k_s = proj1[n_p:, c_w:2 * c_w].reshape(db, ts, C_HEADS, C_DH)
    c_v_s = proj1[n_p:, 2 * c_w:].reshape(db, ts, C_HEADS, C_DH)
    return (y_p, y_s, a_k_p, a_v_p, b_state_p, c_k_p, c_v_p, a_k_s, a_v_s, b_state_s, c_k_s, c_v_s)
```

```python
import functools
import math

import jax
import jax.numpy as jnp
from jax import lax
from jax.experimental import pallas as pl
from jax.experimental.pallas import tpu as pltpu

F32 = jnp.float32
BF16 = jnp.bfloat16

D_MODEL = 1024
EPS = 1e-6
A_HEADS, A_DH, A_DV = 4, 64, 128
LAMBDA_INIT = 0.8 - 0.6 * math.exp(-0.3 * 0)
B_HEADS, B_DK, B_DV = 4, 64, 128
CHUNK = 128
ROPE_BASE = 10000.0
C_HEADS, C_DH = 16, 64
MOE_GROUPS, MOE_EXPERTS, MOE_TOPK = 4, 8, 2
MOE_FF = D_MODEL // 2
N_EXPERTS = MOE_GROUPS * MOE_EXPERTS

LANES = 128
ROW_TILE = 256
DMA_UNROLL = 8
ATT_TILE_A = 512
ATT_TILE_CQ, ATT_TILE_CK = 512, 256
PAGE_GROUP = 16
LOG2E = math.log2(math.e)
STICK_DEAD = -104.0 * LOG2E
VMEM_LIMIT = 56 * 1024 * 1024


def _params(sem, vmem=VMEM_LIMIT):
    return pltpu.CompilerParams(dimension_semantics=sem, vmem_limit_bytes=vmem)


def _dot(a, b):
    return jnp.dot(a, b, preferred_element_type=F32)


def _dot_nt(a, b):
    return lax.dot_general(a, b, (((1,), (1,)), ((), ())), preferred_element_type=F32)


def _dot_tn(a, b):
    return lax.dot_general(a, b, (((0,), (0,)), ((), ())), preferred_element_type=F32)


def _rms(x):
    return x * lax.rsqrt(jnp.mean(x * x, axis=-1, keepdims=True) + EPS)


def _norm_proj_kernel(x_ref, g_ref, w_ref, o_ref, *extra_refs, head_cols, t_cols, t_tiles):
    xn = _rms(x_ref[...]) * g_ref[...]
    o = _dot(xn.astype(BF16), w_ref[...])
    o_ref[...] = o
    head_refs, t_refs = extra_refs[:len(head_cols)], extra_refs[len(head_cols):]
    for ref, (col, heads) in zip(head_refs, head_cols):
        for hh in range(heads):
            ref[pl.ds(hh, ROW_TILE, stride=heads), :] = o[:, col + hh * LANES:col + (hh + 1) * LANES]

    @pl.when(pl.program_id(0) < t_tiles)
    def _():
        for ref, (col, width) in zip(t_refs, t_cols):
            ref[0] = o[:, col:col + width].T


def _norm_proj(x, g, w, head_cols=(), t_cols=(), t_rows=0, t_len=ROW_TILE):
    n, d = x.shape
    nout = w.shape[1]
    out_specs = [pl.BlockSpec((ROW_TILE, nout), lambda i: (i, 0))]
    out_shape = [jax.ShapeDtypeStruct((n, nout), F32)]
    for _, heads in head_cols:
        out_specs.append(pl.BlockSpec((ROW_TILE * heads, LANES), lambda i: (i, 0)))
        out_shape.append(jax.ShapeDtypeStruct((n * heads, LANES), F32))
    t_tiles = t_rows // ROW_TILE
    per_seq = t_len // ROW_TILE

    def t_index(i):
        j = jnp.minimum(i, t_tiles - 1)
        return j // per_seq, 0, j % per_seq

    for _, width in t_cols:
        out_specs.append(pl.BlockSpec((1, width, ROW_TILE), t_index))
        out_shape.append(jax.ShapeDtypeStruct((t_rows // t_len, width, t_len), F32))
    return pl.pallas_call(
        functools.partial(_norm_proj_kernel, head_cols=tuple(head_cols), t_cols=tuple(t_cols), t_tiles=t_tiles),
        grid=(n // ROW_TILE,),
        in_specs=[
            pl.BlockSpec((ROW_TILE, d), lambda i: (i, 0)),
            pl.BlockSpec((1, d), lambda i: (0, 0)),
            pl.BlockSpec((d, nout), lambda i: (0, 0)),
        ],
        out_specs=out_specs,
        out_shape=out_shape,
        compiler_params=_params(("arbitrary",)),
        name="norm_proj",
    )(x, g.reshape(1, d), w)


def _route(logits):
    lane = lax.broadcasted_iota(jnp.int32, logits.shape, 1).astype(F32)
    neg = jnp.float32(-jnp.inf)
    big = jnp.float32(LANES)
    gl = jnp.where(lane < MOE_GROUPS, logits, neg)
    gmax = jnp.max(gl, axis=-1, keepdims=True)
    g_idx = jnp.min(jnp.where(gl == gmax, lane, big), axis=-1, keepdims=True)
    g_w = 1.0 / jnp.sum(jnp.exp(gl - gmax), axis=-1, keepdims=True)
    lo = MOE_GROUPS + g_idx * MOE_EXPERTS
    el = jnp.where((lane >= lo) & (lane < lo + MOE_EXPERTS), logits, neg)
    v1 = jnp.max(el, axis=-1, keepdims=True)
    i1 = jnp.min(jnp.where(el == v1, lane, big), axis=-1, keepdims=True)
    el2 = jnp.where(lane == i1, neg, el)
    v2 = jnp.max(el2, axis=-1, keepdims=True)
    i2 = jnp.min(jnp.where(el2 == v2, lane, big), axis=-1, keepdims=True)
    e2 = jnp.exp(v2 - v1)
    w1 = g_w / (1.0 + e2)
    w2 = g_w * e2 / (1.0 + e2)
    ids = jnp.where(lane == 0, i1 - MOE_GROUPS, jnp.where(lane == 1, i2 - MOE_GROUPS, 0.0))
    wts = jnp.where(lane == 0, w1, jnp.where(lane == 1, w2, 0.0))
    return ids.astype(jnp.int32), wts


def _out_route_kernel(*refs, n_a, p_tiles):
    a_refs = refs[:2 * n_a]
    w_refs = refs[2 * n_a:3 * n_a]
    h_ref, g_ref, wrh_ref, wrl_ref, br_ref, ho_ref, xn_ref, ids_ref, wts_ref = refs[3 * n_a:]
    is_prompt = pl.program_id(0) < p_tiles
    h = h_ref[...]
    for j, w_ref in enumerate(w_refs):
        a = jnp.where(is_prompt, a_refs[2 * j][...], a_refs[2 * j + 1][...])
        h = h + _dot(a.astype(BF16), w_ref[...])
    ho_ref[...] = h
    xn = _rms(h) * g_ref[...]
    xn_ref[...] = xn
    xh, xl = _split_bf16(xn)
    logits = _dot(xh, wrh_ref[...]) + _dot(xl, wrh_ref[...]) + _dot(xh, wrl_ref[...]) + br_ref[...]
    ids, wts = _route(logits)
    ids_ref[...] = ids
    wts_ref[...] = wts


def _out_route(a_list, w_list, h, g, wr, br):
    n, d = h.shape
    n_a = len(a_list)
    p_tiles = a_list[0][0].shape[0] // ROW_TILE
    row = lambda i: (i, 0)
    const = lambda i: (0, 0)
    in_specs = []
    for a_p, a_s in a_list:
        in_specs.append(pl.BlockSpec((ROW_TILE, a_p.shape[1]), lambda i: (jnp.minimum(i, p_tiles - 1), 0)))
        in_specs.append(pl.BlockSpec((ROW_TILE, a_s.shape[1]), lambda i: (jnp.maximum(i - p_tiles, 0), 0)))
    in_specs += [pl.BlockSpec(w.shape, const) for w in w_list]
    in_specs += [pl.BlockSpec((ROW_TILE, d), row), pl.BlockSpec((1, d), const),
                 pl.BlockSpec((d, LANES), const), pl.BlockSpec((d, LANES), const), pl.BlockSpec((1, LANES), const)]
    wr_hi = wr.astype(BF16)
    wr_lo = (wr - wr_hi.astype(F32)).astype(BF16)
    return pl.pallas_call(
        functools.partial(_out_route_kernel, n_a=n_a, p_tiles=p_tiles),
        grid=(n // ROW_TILE,),
        in_specs=in_specs,
        out_specs=[pl.BlockSpec((ROW_TILE, d), row), pl.BlockSpec((ROW_TILE, d), row),
                   pl.BlockSpec((ROW_TILE, LANES), row), pl.BlockSpec((ROW_TILE, LANES), row)],
        out_shape=[jax.ShapeDtypeStruct((n, d), F32), jax.ShapeDtypeStruct((n, d), F32),
                   jax.ShapeDtypeStruct((n, LANES), jnp.int32), jax.ShapeDtypeStruct((n, LANES), F32)],
        compiler_params=_params(("parallel",)),
        name="out_route",
    )(*[a for pair in a_list for a in pair], *w_list, h, g.reshape(1, d), wr_hi, wr_lo, br)


def _moe_dispatch_kernel(pos_ref, last_ref, nv_ref, x_ref, xs_hbm, zero_scr, sem):
    i = pl.program_id(0)
    n_tiles = xs_hbm.shape[0] // ROW_TILE

    def zero_tile(t):
        start = pl.multiple_of(t * ROW_TILE, ROW_TILE)
        return pltpu.make_async_copy(zero_scr, xs_hbm.at[pl.ds(start, ROW_TILE), :], sem)

    @pl.when(i == 0)
    def _():
        zero_scr[...] = jnp.zeros(zero_scr.shape, F32)
        for e in range(N_EXPERTS):
            @pl.when(last_ref[e] >= 0)
            def _():
                zero_tile(last_ref[e]).start()

        def fill(t, c):
            zero_tile(t).start()
            return c

        lax.fori_loop(nv_ref[0], n_tiles, fill, 0)
        for e in range(N_EXPERTS):
            @pl.when(last_ref[e] >= 0)
            def _():
                zero_tile(0).wait()

        def drain(t, c):
            zero_tile(0).wait()
            return c

        lax.fori_loop(nv_ref[0], n_tiles, drain, 0)

    def issue(r, c):
        for k in range(MOE_TOPK):
            dst = pos_ref[(i * ROW_TILE + r) * MOE_TOPK + k]
            pltpu.make_async_copy(x_ref.at[pl.ds(r, 1), :], xs_hbm.at[pl.ds(dst, 1), :], sem).start(priority=k)
        return c

    lax.fori_loop(0, ROW_TILE, issue, 0, unroll=DMA_UNROLL)
    for k in range(MOE_TOPK):
        pltpu.make_async_copy(x_ref, xs_hbm.at[pl.ds(0, ROW_TILE), :], sem).wait()


def _moe_dispatch(xn, pos, last_tile, n_valid, n_rows):
    n, d = xn.shape
    grid_spec = pltpu.PrefetchScalarGridSpec(
        num_scalar_prefetch=3,
        grid=(n // ROW_TILE,),
        in_specs=[pl.BlockSpec((ROW_TILE, d), lambda i, pos, last, nv: (i, 0))],
        out_specs=pl.BlockSpec(memory_space=pl.ANY),
        scratch_shapes=[pltpu.VMEM((ROW_TILE, d), F32), pltpu.SemaphoreType.DMA(())],
    )
    return pl.pallas_call(
        _moe_dispatch_kernel,
        grid_spec=grid_spec,
        out_shape=jax.ShapeDtypeStruct((n_rows, d), F32),
        compiler_params=_params(("arbitrary",)),
        name="moe_dispatch",
    )(pos, last_tile, n_valid, xn)


def _moe_ffn_kernel(te_ref, nv_ref, x_ref, wg_ref, wu_ref, wd_ref, o_ref):
    t = pl.program_id(0)

    @pl.when(t < nv_ref[0])
    def _():
        x = x_ref[...].astype(BF16)
        hid = jax.nn.silu(_dot(x, wg_ref[0].astype(BF16))) * _dot(x, wu_ref[0].astype(BF16))
        o_ref[...] = _dot(hid.astype(BF16), wd_ref[0].astype(BF16))

    @pl.when(t >= nv_ref[0])
    def _():
        o_ref[...] = jnp.zeros_like(o_ref)


def _moe_ffn(x_sorted, tile_expert, n_valid, w_gate, w_up, w_down):
    p, d = x_sorted.shape
    f = w_gate.shape[2]
    used = lambda t, te, nv: (jnp.maximum(jnp.minimum(t, nv[0] - 1), 0), 0)
    grid_spec = pltpu.PrefetchScalarGridSpec(
        num_scalar_prefetch=2,
        grid=(p // ROW_TILE,),
        in_specs=[
            pl.BlockSpec((ROW_TILE, d), used),
            pl.BlockSpec((1, d, f), lambda t, te, nv: (te[t], 0, 0)),
            pl.BlockSpec((1, d, f), lambda t, te, nv: (te[t], 0, 0)),
            pl.BlockSpec((1, f, d), lambda t, te, nv: (te[t], 0, 0)),
        ],
        out_specs=pl.BlockSpec((ROW_TILE, d), lambda t, te, nv: (t, 0)),
    )
    return pl.pallas_call(
        _moe_ffn_kernel,
        grid_spec=grid_spec,
        out_shape=jax.ShapeDtypeStruct((p, d), F32),
        compiler_params=_params(("arbitrary",)),
        name="moe_ffn",
    )(tile_expert, n_valid, x_sorted, w_gate, w_up, w_down)


def _moe_combine_kernel(pos_ref, h_ref, w_ref, *refs, final_norm, p_tiles):
    g_ref = refs[0] if final_norm else None
    y_hbm = refs[1 if final_norm else 0]
    rbuf, sem = refs[-2:]
    o_refs = refs[(2 if final_norm else 1):-2]
    i = pl.program_id(0)

    def gather(tile, slot):
        def issue(r, c):
            for k in range(MOE_TOPK):
                src = pos_ref[(tile * ROW_TILE + r) * MOE_TOPK + k]
                pltpu.make_async_copy(y_hbm.at[pl.ds(src, 1), :], rbuf.at[slot, k, pl.ds(r, 1), :],
                                      sem.at[slot]).start(priority=k)
            return c

        lax.fori_loop(0, ROW_TILE, issue, 0, unroll=DMA_UNROLL)

    @pl.when(i == 0)
    def _():
        gather(0, 0)

    @pl.when(i + 1 < pl.num_programs(0))
    def _():
        gather(i + 1, (i + 1) % 2)

    slot = i % 2
    for k in range(MOE_TOPK):
        pltpu.make_async_copy(y_hbm.at[pl.ds(0, ROW_TILE), :], rbuf.at[slot, k], sem.at[slot]).wait()
    w = w_ref[...]
    acc = h_ref[...]
    for k in range(MOE_TOPK):
        acc = acc + w[:, k:k + 1] * rbuf[slot, k]
    if not final_norm:
        o_refs[0][...] = acc
    else:
        acc = _rms(acc) * g_ref[...]

        @pl.when(i < p_tiles)
        def _():
            o_refs[0][...] = acc

        @pl.when(i >= p_tiles)
        def _():
            o_refs[1][...] = acc


def _moe_combine(h, wts, y_sorted, pos, gain=None, n_prompt=0):
    n, d = h.shape
    row = lambda i, pos: (i, 0)
    in_specs = [pl.BlockSpec((ROW_TILE, d), row), pl.BlockSpec((ROW_TILE, LANES), row)]
    args = [h, wts]
    p_tiles = n_prompt // ROW_TILE
    if gain is None:
        out_specs = [pl.BlockSpec((ROW_TILE, d), row)]
        out_shape = [jax.ShapeDtypeStruct((n, d), F32)]
    else:
        in_specs.append(pl.BlockSpec((1, d), lambda i, pos: (0, 0)))
        args.append(gain.reshape(1, d))
        out_specs = [pl.BlockSpec((ROW_TILE, d), lambda i, pos: (jnp.minimum(i, p_tiles - 1), 0)),
                     pl.BlockSpec((ROW_TILE, d), lambda i, pos: (jnp.maximum(i - p_tiles, 0), 0))]
        out_shape = [jax.ShapeDtypeStruct((n_prompt, d), F32), jax.ShapeDtypeStruct((n - n_prompt, d), F32)]
    grid_spec = pltpu.PrefetchScalarGridSpec(
        num_scalar_prefetch=1,
        grid=(n // ROW_TILE,),
        in_specs=in_specs + [pl.BlockSpec(memory_space=pl.ANY)],
        out_specs=out_specs,
        scratch_shapes=[pltpu.VMEM((2, MOE_TOPK, ROW_TILE, d), F32), pltpu.SemaphoreType.DMA((2,))],
    )
    return pl.pallas_call(
        functools.partial(_moe_combine_kernel, final_norm=gain is not None, p_tiles=p_tiles),
        grid_spec=grid_spec,
        out_shape=out_shape,
        compiler_params=_params(("arbitrary",)),
        name="moe_combine",
    )(pos, *args, y_sorted)


def _dispatch_plan(ids):
    n = ids.shape[0]
    a = n * MOE_TOPK
    n_tiles = a // ROW_TILE + N_EXPERTS
    e_ids = jnp.arange(N_EXPERTS, dtype=jnp.int32)
    onehot = (ids.reshape(a, 1) == e_ids[None, :]).astype(jnp.int32)
    seen = jnp.cumsum(onehot, axis=0)
    counts = seen[-1]
    tiles_per = (counts + ROW_TILE - 1) // ROW_TILE
    tile_end = jnp.cumsum(tiles_per)
    tile_start = tile_end - tiles_per
    pos = jnp.sum(onehot * (seen - 1 + (tile_start * ROW_TILE)[None, :]), axis=1).astype(jnp.int32)
    n_valid = tile_end[-1].astype(jnp.int32)
    t_idx = jnp.arange(n_tiles, dtype=jnp.int32)
    te = jnp.minimum(jnp.sum((tile_end[None, :] <= t_idx[:, None]).astype(jnp.int32), axis=1), N_EXPERTS - 1)
    last = jnp.max(jnp.where(tiles_per > 0, e_ids, 0))
    tile_expert = jnp.where(t_idx < n_valid, te, last).astype(jnp.int32)
    last_tile = jnp.where(tiles_per > 0, tile_end - 1, -1).astype(jnp.int32)
    return tile_expert, n_valid.reshape(1), pos, last_tile, n_tiles * ROW_TILE


def _moe(h, xn, ids, wts, w_gate, w_up, w_down, final_gain=None, n_prompt=0):
    tile_expert, n_valid, pos, last_tile, n_rows = _dispatch_plan(ids[:, :MOE_TOPK])
    x_sorted = _moe_dispatch(xn, pos, last_tile, n_valid, n_rows)
    y_sorted = _moe_ffn(x_sorted, tile_expert, n_valid, w_gate, w_up, w_down)
    return _moe_combine(h, wts, y_sorted, pos, final_gain, n_prompt)


def _lambda_from(lam_ref):
    lp = lam_ref[...]
    s1 = jnp.sum(lp[0:1, :] * lp[1:2, :], axis=-1, keepdims=True)
    s2 = jnp.sum(lp[2:3, :] * lp[3:4, :], axis=-1, keepdims=True)
    return jnp.exp(s1) - jnp.exp(s2) + LAMBDA_INIT


def _transpose_tiles(src_ref, dst_ref, tile):
    def body(j, c):
        start = pl.multiple_of(j * tile, tile)
        dst_ref[j] = src_ref[pl.ds(start, tile), :].T.astype(BF16)
        return c

    lax.fori_loop(0, dst_ref.shape[0], body, 0)


def _diff_prompt_kernel(slopes_ref, q_ref, k_ref, v_ref, lam_ref, g_ref, o_ref, vt_scr, s_scr, m_scr, l_scr, acc_scr,
                        *, tq, tk):
    h = pl.program_id(1)
    qi = pl.program_id(2)

    @pl.when(qi == 0)
    def _():
        _transpose_tiles(v_ref, vt_scr, tk)

    slope = slopes_ref[h] * LOG2E
    q = q_ref[...] * (A_DH ** -0.5 * LOG2E)
    lane = lax.broadcasted_iota(jnp.int32, q.shape, 1)
    qcat = jnp.concatenate([jnp.where(lane < A_DH, q, 0.0), jnp.where(lane >= A_DH, q, 0.0)], axis=0).astype(BF16)
    key = lax.broadcasted_iota(jnp.int32, (tk, 2 * tq), 0)
    qry = lax.broadcasted_iota(jnp.int32, (tk, 2 * tq), 1) % tq
    bias0 = slope * key.astype(F32)
    m_scr[...] = jnp.full(m_scr.shape, -jnp.inf, F32)
    l_scr[...] = jnp.zeros(l_scr.shape, F32)
    acc_scr[...] = jnp.zeros(acc_scr.shape, F32)

    def scores(j):
        start = pl.multiple_of(j * tk, tk)
        return _dot_nt(k_ref[pl.ds(start, tk), :].astype(BF16), qcat)

    def update(slot, j, mask):
        cb = slope * (j * tk).astype(F32)
        t = s_scr[slot] + bias0
        if mask is not None:
            t = jnp.where(mask, t, -jnp.inf)
        m_old = m_scr[...]
        m_new = jnp.maximum(m_old, jnp.max(t, axis=0, keepdims=True) + cb)
        p = jnp.exp2(t + (cb - m_new))
        alpha = jnp.exp2(m_old - m_new)
        l_scr[...] = alpha * l_scr[...] + jnp.sum(p, axis=0, keepdims=True)
        acc_scr[...] = alpha * acc_scr[...] + _dot(vt_scr[j], p.astype(BF16))
        m_scr[...] = m_new

    s_scr[0] = scores(0)

    def body(i, carry):
        j = 2 * i
        s_scr[1] = scores(j + 1)
        update(0, j, None)
        s_scr[0] = scores(j + 2)
        update(1, j + 1, None)
        return carry

    lax.fori_loop(0, qi // 2, body, 0)
    diag = key <= qry

    @pl.when(qi % 2 == 1)
    def _():
        s_scr[1] = scores(qi)
        update(0, qi - 1, None)
        update(1, qi, diag)

    @pl.when(qi % 2 == 0)
    def _():
        update(0, qi, diag)

    lam = _lambda_from(lam_ref)
    o = acc_scr[...] / l_scr[...]
    o = (o[:, :tq] - lam * o[:, tq:]).T
    o_ref[...] = _rms(o) * g_ref[...] * (1.0 - LAMBDA_INIT)


def _diff_prompt(proj, slopes, lam_pack, subln, n_batch, t_len):
    tq = tk = min(ATT_TILE_A, t_len)
    nq = t_len // tq
    qcol, kcol, vcol = 0, A_HEADS, 2 * A_HEADS
    grid_spec = pltpu.PrefetchScalarGridSpec(
        num_scalar_prefetch=1,
        grid=(n_batch, A_HEADS, nq),
        in_specs=[
            pl.BlockSpec((tq, LANES), lambda b, h, i, s: (b * nq + i, qcol + h)),
            pl.BlockSpec((t_len, LANES), lambda b, h, i, s: (b, kcol + h)),
            pl.BlockSpec((t_len, LANES), lambda b, h, i, s: (b, vcol + h)),
            pl.BlockSpec((8, LANES), lambda b, h, i, s: (0, 0)),
            pl.BlockSpec((1, LANES), lambda b, h, i, s: (0, 0)),
        ],
        out_specs=pl.BlockSpec((tq, LANES), lambda b, h, i, s: (b * nq + i, h)),
        scratch_shapes=[pltpu.VMEM((t_len // tk, LANES, tk), BF16), pltpu.VMEM((2, tk, 2 * tq), F32),
                        pltpu.VMEM((1, 2 * tq), F32), pltpu.VMEM((1, 2 * tq), F32), pltpu.VMEM((LANES, 2 * tq), F32)],
    )
    return pl.pallas_call(
        functools.partial(_diff_prompt_kernel, tq=tq, tk=tk),
        grid_spec=grid_spec,
        out_shape=jax.ShapeDtypeStruct((n_batch * t_len, A_HEADS * A_DV), F32),
        compiler_params=_params(("parallel", "parallel", "arbitrary")),
        name="diff_prompt",
    )(slopes, proj, proj, proj, lam_pack, subln.reshape(1, LANES))


def _diff_sample_kernel(pt_ref, slopes_ref, q_ref, kn_ref, vn_ref, *refs, ts, page, n_pages, group):
    kp_refs, vp_refs = refs[:group], refs[group:2 * group]
    lam_ref, g_ref, o_ref, m_scr, l_scr, acc_scr = refs[2 * group:]
    p = pl.program_id(1)
    hrows = 2 * ts
    rows = A_HEADS * hrows
    past = n_pages * page
    q = q_ref[...] * (A_DH ** -0.5 * LOG2E)
    lane = lax.broadcasted_iota(jnp.int32, (ts, LANES), 1)
    qh = []
    for hh in range(A_HEADS):
        qq = q[:, hh * LANES:(hh + 1) * LANES]
        qh.append(jnp.concatenate([jnp.where(lane < A_DH, qq, 0.0), jnp.where(lane >= A_DH, qq, 0.0)], axis=0).astype(BF16))
    rid = lax.broadcasted_iota(jnp.int32, (rows, 1), 0)
    rt = rid % ts
    slope_rows = jnp.zeros((rows, 1), F32)
    for hh in range(A_HEADS):
        slope_rows = jnp.where(rid // hrows == hh, slopes_ref[hh] * LOG2E, slope_rows)

    def update(k_of, v_of, kpos, new_tokens):
        s = jnp.concatenate([jnp.concatenate([_dot_nt(qh[hh], k.astype(BF16)) for k in k_of(hh)], axis=1)
                             for hh in range(A_HEADS)], axis=0)
        t = s + slope_rows * kpos.astype(F32)
        if new_tokens:
            t = jnp.where(kpos - past <= rt, t, -jnp.inf)
        m_old = m_scr[...]
        m_new = jnp.maximum(m_old, jnp.max(t, axis=-1, keepdims=True))
        pr = jnp.exp2(t - m_new)
        alpha = jnp.exp2(m_old - m_new)
        l_scr[...] = alpha * l_scr[...] + jnp.sum(pr, axis=-1, keepdims=True)
        for hh in range(A_HEADS):
            sl = slice(hh * hrows, (hh + 1) * hrows)
            pv, off = 0.0, 0
            for v in v_of(hh):
                pv = pv + _dot(pr[sl, off:off + v.shape[0]].astype(BF16), v.astype(BF16))
                off += v.shape[0]
            acc_scr[hh] = alpha[sl] * acc_scr[hh] + pv
        m_scr[...] = m_new

    @pl.when(p == 0)
    def _():
        m_scr[...] = jnp.full(m_scr.shape, -jnp.inf, F32)
        l_scr[...] = jnp.zeros(l_scr.shape, F32)
        acc_scr[...] = jnp.zeros(acc_scr.shape, F32)
        kpos = past + lax.broadcasted_iota(jnp.int32, (1, ts), 1)
        update(lambda hh: [kn_ref[:, hh * LANES:(hh + 1) * LANES]], lambda hh: [vn_ref[:, hh * A_DV:(hh + 1) * A_DV]],
               kpos, True)

    kpos = p * (group * page) + lax.broadcasted_iota(jnp.int32, (1, group * page), 1)
    update(lambda hh: [r[0, pl.ds(hh, page, stride=A_HEADS), :] for r in kp_refs],
           lambda hh: [r[0, pl.ds(hh, page, stride=A_HEADS), :] for r in vp_refs], kpos, False)

    @pl.when(p == n_pages // group - 1)
    def _():
        lam = _lambda_from(lam_ref)
        g = g_ref[...]
        inv_l = 1.0 / l_scr[...]
        outs = []
        for hh in range(A_HEADS):
            o = acc_scr[hh] * inv_l[hh * hrows:(hh + 1) * hrows]
            o = o[:ts] - lam * o[ts:]
            outs.append(_rms(o) * g * (1.0 - LAMBDA_INIT))
        o_ref[...] = jnp.concatenate(outs, axis=-1)


def _diff_sample(proj, row0, cache_k, cache_v, page_table, slopes, lam_pack, subln, n_seq, ts):
    n_pages = page_table.shape[1]
    page = cache_k.shape[1]
    width = A_HEADS * 2 * A_DH
    vwidth = A_HEADS * A_DV
    rb = row0 // ts
    rows = 2 * A_HEADS * ts
    qblk, kblk, vblk = 0, 1, 2
    group = math.gcd(n_pages, PAGE_GROUP)

    def page_spec(i, w):
        return pl.BlockSpec((1, page * A_HEADS, w), lambda s, p, pt, sl: (pt[s, p * group + i], 0, 0))

    grid_spec = pltpu.PrefetchScalarGridSpec(
        num_scalar_prefetch=2,
        grid=(n_seq, n_pages // group),
        in_specs=[
            pl.BlockSpec((ts, width), lambda s, p, pt, sl: (rb + s, qblk)),
            pl.BlockSpec((ts, width), lambda s, p, pt, sl: (rb + s, kblk)),
            pl.BlockSpec((ts, vwidth), lambda s, p, pt, sl: (rb + s, vblk)),
            *[page_spec(i, 2 * A_DH) for i in range(group)],
            *[page_spec(i, A_DV) for i in range(group)],
            pl.BlockSpec((8, LANES), lambda s, p, pt, sl: (0, 0)),
            pl.BlockSpec((1, LANES), lambda s, p, pt, sl: (0, 0)),
        ],
        out_specs=pl.BlockSpec((ts, vwidth), lambda s, p, pt, sl: (s, 0)),
        scratch_shapes=[pltpu.VMEM((rows, 1), F32), pltpu.VMEM((rows, 1), F32),
                        pltpu.VMEM((A_HEADS, 2 * ts, A_DV), F32)],
    )
    k_pages = cache_k.reshape(-1, page * A_HEADS, 2 * A_DH)
    v_pages = cache_v.reshape(-1, page * A_HEADS, A_DV)
    return pl.pallas_call(
        functools.partial(_diff_sample_kernel, ts=ts, page=page, n_pages=n_pages, group=group),
        grid_spec=grid_spec,
        out_shape=jax.ShapeDtypeStruct((n_seq * ts, vwidth), F32),
        compiler_params=_params(("parallel", "arbitrary")),
        name="diff_sample",
    )(page_table, slopes, proj, proj, proj, *([k_pages] * group), *([v_pages] * group), lam_pack,
      subln.reshape(1, LANES))


def _retention_kernel(q_ref, k_ref, v_ref, g_ref, cos_ref, sin_ref, s0_ref, o_ref, s_out_ref, s_scr, *, chunk):
    c = pl.program_id(1)

    @pl.when(c == 0)
    def _():
        s_scr[...] = s0_ref[0]

    cos = cos_ref[...]
    sin = sin_ref[...]
    width = B_HEADS * B_DK
    lane = lax.broadcasted_iota(jnp.int32, (chunk, width), 1)
    first_half = (lane % B_DK) < (B_DK // 2)

    def rope(x):
        partner = jnp.where(first_half, pltpu.roll(x, width - B_DK // 2, 1), pltpu.roll(x, B_DK // 2, 1))
        return x * cos + partner * sin

    q = rope(q_ref[...])
    k = rope(k_ref[...]) * (B_DK ** -0.5)
    v = v_ref[...]
    g = g_ref[...]
    n_col = lax.broadcasted_iota(jnp.int32, (chunk, 1), 0).astype(F32)
    ri = lax.broadcasted_iota(jnp.int32, (chunk, chunk), 0)
    ci = lax.broadcasted_iota(jnp.int32, (chunk, chunk), 1)
    dist = (ri - ci).astype(F32)
    outs = []
    for hh in range(B_HEADS):
        log_g = math.log1p(-(2.0 ** (-5.0 - hh)))
        decay = jnp.where(ri >= ci, jnp.exp(log_g * jnp.maximum(dist, 0.0)), 0.0)
        qh = q[:, hh * B_DK:(hh + 1) * B_DK]
        kh = k[:, hh * B_DK:(hh + 1) * B_DK]
        vh = v[:, hh * B_DV:(hh + 1) * B_DV].astype(BF16)
        s_h = s_scr[hh]
        inner = _dot_nt(qh.astype(BF16), kh.astype(BF16)) * decay
        q_dec = qh * jnp.exp(log_g * (n_col + 1.0))
        o = _dot(inner.astype(BF16), vh) + _dot(q_dec.astype(BF16), s_h.astype(BF16))
        k_dec = kh * jnp.exp(log_g * (chunk - 1.0 - n_col))
        s_scr[hh] = math.exp(log_g * chunk) * s_h + _dot_tn(k_dec.astype(BF16), vh)
        gh = g[:, hh * B_DV:(hh + 1) * B_DV]
        outs.append(jax.nn.silu(gh) * _rms(o))
    o_ref[...] = jnp.concatenate(outs, axis=-1)

    @pl.when(c == pl.num_programs(1) - 1)
    def _():
        s_out_ref[0] = s_scr[...]


def _retention(proj, row0, cos_t, sin_t, s0, n_seq, t_len):
    chunk = CHUNK if (t_len > CHUNK and t_len % CHUNK == 0) else t_len
    nc = t_len // chunk
    rb = row0 // chunk
    qw, vw = B_HEADS * B_DK, B_HEADS * B_DV
    qcol, kcol, vcol, gcol = 6, 7, 4, 5
    row = lambda s, c: (rb + s * nc + c, 0)
    return pl.pallas_call(
        functools.partial(_retention_kernel, chunk=chunk),
        grid=(n_seq, nc),
        in_specs=[
            pl.BlockSpec((chunk, qw), lambda s, c: (rb + s * nc + c, qcol)),
            pl.BlockSpec((chunk, qw), lambda s, c: (rb + s * nc + c, kcol)),
            pl.BlockSpec((chunk, vw), lambda s, c: (rb + s * nc + c, vcol)),
            pl.BlockSpec((chunk, vw), lambda s, c: (rb + s * nc + c, gcol)),
            pl.BlockSpec((chunk, qw), lambda s, c: (c, 0)),
            pl.BlockSpec((chunk, qw), lambda s, c: (c, 0)),
            pl.BlockSpec((1, B_HEADS, B_DK, B_DV), lambda s, c: (s, 0, 0, 0)),
        ],
        out_specs=[pl.BlockSpec((chunk, vw), lambda s, c: (s * nc + c, 0)),
                   pl.BlockSpec((1, B_HEADS, B_DK, B_DV), lambda s, c: (s, 0, 0, 0))],
        out_shape=[jax.ShapeDtypeStruct((n_seq * t_len, vw), F32),
                   jax.ShapeDtypeStruct((n_seq, B_HEADS, B_DK, B_DV), F32)],
        scratch_shapes=[pltpu.VMEM((B_HEADS, B_DK, B_DV), F32)],
        compiler_params=_params(("parallel", "arbitrary")),
        name="retention",
    )(proj, proj, proj, proj, cos_t, sin_t, s0)


def _rope_tables(pos):
    half = B_DK // 2
    inv = ROPE_BASE ** (-jnp.arange(half, dtype=F32) / half)
    ang = pos.astype(F32)[:, None] * inv[None, :]
    cos, sin = jnp.cos(ang), jnp.sin(ang)
    cos_t = jnp.tile(jnp.concatenate([cos, cos], axis=-1), (1, B_HEADS))
    sin_t = jnp.tile(jnp.concatenate([-sin, sin], axis=-1), (1, B_HEADS))
    return cos_t, sin_t


def _softplus2(z2):
    return jnp.maximum(z2, 0.0) + jnp.log2(1.0 + jnp.exp2(-jnp.abs(z2)))


def _split_bf16(x):
    hi = x.astype(BF16)
    return hi, (x - hi.astype(F32)).astype(BF16)


def _stick_weights_t(zt, carry, newer_mat, mask):
    sp = _softplus2(zt)
    lk = -sp
    if mask is not None:
        lk = jnp.where(mask, lk, 0.0)
    hi, lo = _split_bf16(lk)
    newer = _dot(newer_mat, hi) + _dot(newer_mat, lo)
    a = jnp.exp2((zt - sp) + newer + carry)
    if mask is not None:
        a = jnp.where(mask, a, 0.0)
    return a, carry + jnp.sum(lk, axis=0, keepdims=True)


def _stick_prompt_kernel(q_ref, k_ref, v_ref, o_ref, vt_scr, carry_scr, acc_scr, *, tq, tk):
    qi = pl.program_id(2)

    @pl.when(qi == 0)
    def _():
        _transpose_tiles(v_ref, vt_scr, tk)

    heads = LANES // C_DH
    width = heads * tq
    q = q_ref[...] * (C_DH ** -0.5 * LOG2E)
    lane = lax.broadcasted_iota(jnp.int32, q.shape, 1)
    qcat = jnp.concatenate([jnp.where(lane // C_DH == hh, q, 0.0) for hh in range(heads)], axis=0).astype(BF16)
    key = lax.broadcasted_iota(jnp.int32, (tk, width), 0)
    qry = lax.broadcasted_iota(jnp.int32, (tk, width), 1) % tq
    rj = lax.broadcasted_iota(jnp.int32, (tk, tk), 0)
    cs = lax.broadcasted_iota(jnp.int32, (tk, tk), 1)
    newer_mat = jnp.where(cs > rj, 1.0, 0.0).astype(BF16)
    sub = tq // tk

    def block(j, mask, carry):
        start = pl.multiple_of(j * tk, tk)
        zt = _dot_nt(k_ref[pl.ds(start, tk), :].astype(BF16), qcat)
        a, carry = _stick_weights_t(zt, carry, newer_mat, mask)
        return _dot(vt_scr[j], a.astype(BF16)), carry

    carry = jnp.zeros((1, width), F32)
    for jj in reversed(range(sub)):
        av, carry = block(qi * sub + jj, (jj * tk + key) < qry, carry)
        if jj == sub - 1:
            acc_scr[...] = av
        else:
            acc_scr[...] += av
    carry_scr[...] = carry

    def cond(state):
        j, cmax = state
        return (j >= 0) & (cmax > STICK_DEAD)

    def body(state):
        j, _ = state
        av, carry = block(j, None, carry_scr[...])
        acc_scr[...] += av
        carry_scr[...] = carry
        return j - 1, jnp.max(carry)

    lax.while_loop(cond, body, (qi * sub - 1, jnp.max(carry)))
    acc = acc_scr[...]
    o_t = jnp.concatenate([acc[hh * C_DH:(hh + 1) * C_DH, hh * tq:(hh + 1) * tq] for hh in range(heads)], axis=0)
    o_ref[...] = o_t.T


def _stick_prompt(proj, n_batch, t_len):
    tq = min(ATT_TILE_CQ, t_len)
    tk = min(ATT_TILE_CK, tq)
    nq = t_len // tq
    ng = C_HEADS * C_DH // LANES
    return pl.pallas_call(
        functools.partial(_stick_prompt_kernel, tq=tq, tk=tk),
        grid=(n_batch, ng, nq),
        in_specs=[
            pl.BlockSpec((tq, LANES), lambda b, g, i: (b * nq + i, g)),
            pl.BlockSpec((t_len, LANES), lambda b, g, i: (b, ng + g)),
            pl.BlockSpec((t_len, LANES), lambda b, g, i: (b, 2 * ng + g)),
        ],
        out_specs=pl.BlockSpec((tq, LANES), lambda b, g, i: (b * nq + i, g)),
        out_shape=jax.ShapeDtypeStruct((n_batch * t_len, C_HEADS * C_DH), F32),
        scratch_shapes=[pltpu.VMEM((t_len // tk, LANES, tk), BF16), pltpu.VMEM((1, LANES // C_DH * tq), F32),
                        pltpu.VMEM((LANES, LANES // C_DH * tq), F32)],
        compiler_params=_params(("parallel", "parallel", "arbitrary")),
        name="stick_prompt",
    )(proj, proj, proj)


def _stick_sample_kernel(pt_ref, q_ref, kn_ref, vn_ref, ck_hbm, cv_hbm, o_ref, kbuf, vbuf, ksem, vsem,
                         carry_scr, acc_scr, *, ts, page, n_pages):
    s = pl.program_id(0)
    rows = C_HEADS * ts

    def copies(seq, i, slot):
        pid = pt_ref[seq, n_pages - 1 - i]
        return (pltpu.make_async_copy(ck_hbm.at[pid], kbuf.at[slot], ksem.at[slot]),
                pltpu.make_async_copy(cv_hbm.at[pid], vbuf.at[slot], vsem.at[slot]))

    def start(seq, i, slot):
        for cp in copies(seq, i, slot):
            cp.start()

    def wait(seq, i, slot):
        for cp in copies(seq, i, slot):
            cp.wait()

    @pl.when(s == 0)
    def _():
        start(0, 0, 0)
        start(0, 1, 1)

    q = q_ref[...] * (C_DH ** -0.5 * LOG2E)
    qh = [q[:, hh * C_DH:(hh + 1) * C_DH].astype(BF16) for hh in range(C_HEADS)]
    r = lax.broadcasted_iota(jnp.int32, (page, page), 0)
    c = lax.broadcasted_iota(jnp.int32, (page, page), 1)
    upper = jnp.where(r > c, 1.0, 0.0).astype(BF16)

    def tile(z_of, av_of, carry, mask):
        z = jnp.concatenate([z_of(hh) for hh in range(C_HEADS)], axis=0)
        sp = _softplus2(z)
        lk = -sp
        if mask is not None:
            lk = jnp.where(mask, lk, 0.0)
        hi, lo = _split_bf16(lk)
        newer = _dot(hi, upper) + _dot(lo, upper)
        a = jnp.exp2((z - sp) + newer + carry)
        if mask is not None:
            a = jnp.where(mask, a, 0.0)
        for hh in range(C_HEADS):
            acc_scr[hh] += av_of(hh, a[hh * ts:(hh + 1) * ts].astype(BF16))
        return carry + jnp.sum(lk, axis=-1, keepdims=True)

    acc_scr[...] = jnp.zeros(acc_scr.shape, F32)
    pad = jnp.zeros((page - ts, C_DH), F32)
    rt = lax.broadcasted_iota(jnp.int32, (rows, page), 0) % ts
    kc = lax.broadcasted_iota(jnp.int32, (rows, page), 1)

    def new_rows(ref, hh):
        return jnp.concatenate([ref[:, hh * C_DH:(hh + 1) * C_DH], pad], axis=0).astype(BF16)

    carry = tile(lambda hh: _dot_nt(qh[hh], new_rows(kn_ref, hh)), lambda hh, a: _dot(a, new_rows(vn_ref, hh)),
                 jnp.zeros((rows, 1), F32), kc < rt)

    def page_tile(slot, carry):
        return tile(lambda hh: _dot(qh[hh], kbuf[slot, hh].astype(BF16)),
                    lambda hh, a: _dot_nt(a, vbuf[slot, hh].astype(BF16)), carry, None)

    wait(s, 0, 0)
    carry = page_tile(0, carry)
    cmax = jnp.max(carry)
    carry_scr[...] = carry

    @pl.when((2 < n_pages) & (cmax > STICK_DEAD))
    def _():
        start(s, 2, 0)

    def cond(state):
        i, cmax = state
        return (i < n_pages) & (cmax > STICK_DEAD)

    def body(state):
        i, _ = state
        slot = i % 2
        wait(s, i, slot)
        carry = page_tile(slot, carry_scr[...])
        cmax = jnp.max(carry)
        carry_scr[...] = carry

        @pl.when((i + 2 < n_pages) & (cmax > STICK_DEAD))
        def _():
            start(s, i + 2, slot)

        return i + 1, cmax

    i_end, _ = lax.while_loop(cond, body, (jnp.int32(1), cmax))

    @pl.when(i_end < n_pages)
    def _():
        wait(s, i_end, i_end % 2)

    @pl.when(s + 1 < pl.num_programs(0))
    def _():
        start(s + 1, 0, 0)
        start(s + 1, 1, 1)

    o_ref[0] = acc_scr[...]


def _stick_sample(proj, row0, cache_k, cache_v, page_table, n_seq, ts):
    n_pages = page_table.shape[1]
    page = cache_k.shape[1]
    assert n_pages >= 2
    width = C_HEADS * C_DH
    rb = row0 // ts
    grid_spec = pltpu.PrefetchScalarGridSpec(
        num_scalar_prefetch=1,
        grid=(n_seq,),
        in_specs=[
            pl.BlockSpec((ts, width), lambda s, pt: (rb + s, 0)),
            pl.BlockSpec((ts, width), lambda s, pt: (rb + s, 1)),
            pl.BlockSpec((ts, width), lambda s, pt: (rb + s, 2)),
            pl.BlockSpec(memory_space=pl.ANY),
            pl.BlockSpec(memory_space=pl.ANY),
        ],
        out_specs=pl.BlockSpec((1, C_HEADS, ts, C_DH), lambda s, pt: (s, 0, 0, 0)),
        scratch_shapes=[pltpu.VMEM((2, C_HEADS, C_DH, page), F32), pltpu.VMEM((2, C_HEADS, C_DH, page), F32),
                        pltpu.SemaphoreType.DMA((2,)), pltpu.SemaphoreType.DMA((2,)),
                        pltpu.VMEM((C_HEADS * ts, 1), F32), pltpu.VMEM((C_HEADS, ts, C_DH), F32)],
    )
    out = pl.pallas_call(
        functools.partial(_stick_sample_kernel, ts=ts, page=page, n_pages=n_pages),
        grid_spec=grid_spec,
        out_shape=jax.ShapeDtypeStruct((n_seq, C_HEADS, ts, C_DH), F32),
        compiler_params=_params(("arbitrary",)),
        name="stick_sample",
    )(page_table, proj, proj, proj, jnp.transpose(cache_k, (0, 2, 3, 1)), jnp.transpose(cache_v, (0, 2, 3, 1)))
    return jnp.transpose(out, (0, 2, 1, 3)).reshape(n_seq * ts, width)


def _router_pack(wg, bg, we, be):
    d = wg.shape[0]
    we_flat = jnp.transpose(we, (1, 0, 2)).reshape(d, N_EXPERTS)
    w = jnp.concatenate([wg, we_flat], axis=1)
    b = jnp.concatenate([bg, be.reshape(N_EXPERTS)])
    pad = LANES - w.shape[1]
    return jnp.pad(w, ((0, 0), (0, pad))), jnp.pad(b, (0, pad)).reshape(1, LANES)


def _expert_weights(w_gate, w_up, w_down):
    d, f = w_gate.shape[2], w_gate.shape[3]
    return w_gate.reshape(N_EXPERTS, d, f), w_up.reshape(N_EXPERTS, d, f), w_down.reshape(N_EXPERTS, f, d)


def kernel(x_prompt, x_sample, cache_a_k, cache_a_v, state_b, cache_c_k, cache_c_v, page_table, ln0_mix, w_in0, lam_q1, lam_k1, lam_q2, lam_k2, a_subln, w_out0, ln0_ffn, moe0_wg, moe0_bg, moe0_we, moe0_be, moe0_w_gate, moe0_w_up, moe0_w_down, ln1_mix, w_in1, w_out1, ln1_ffn, moe1_wg, moe1_bg, moe1_we, moe1_be, moe1_w_gate, moe1_w_up, moe1_w_down, ln_f):
    bp, tp, d = x_prompt.shape
    db, ts, _ = x_sample.shape
    n_p, n_s = bp * tp, db * ts
    past = page_table.shape[1] * cache_a_k.shape[1]
    h = jnp.concatenate([x_prompt.reshape(n_p, d), x_sample.reshape(n_s, d)], axis=0)

    slopes = jnp.exp2(-8.0 * jnp.arange(1, A_HEADS + 1, dtype=F32) / A_HEADS)
    lam_pack = jnp.pad(jnp.stack([lam_q1, lam_k1, lam_q2, lam_k2]), ((0, 4), (0, LANES - A_DH)))
    cos_p, sin_p = _rope_tables(jnp.arange(tp, dtype=jnp.int32))
    cos_s, sin_s = _rope_tables(past + jnp.arange(ts, dtype=jnp.int32))

    a_w = A_HEADS * 2 * A_DH
    proj0, a_k_rows, a_v_rows = _norm_proj(h, ln0_mix, w_in0.astype(BF16), ((a_w, A_HEADS), (2 * a_w, A_HEADS)))
    a_o_p = _diff_prompt(proj0, slopes, lam_pack, a_subln, bp, tp)
    a_o_s = _diff_sample(proj0, n_p, cache_a_k, cache_a_v, page_table, slopes, lam_pack, a_subln, db, ts)
    b_o_p, b_state_p = _retention(proj0, 0, cos_p, sin_p, jnp.zeros((bp, B_HEADS, B_DK, B_DV), F32), bp, tp)
    b_o_s, b_state_s = _retention(proj0, n_p, cos_s, sin_s, state_b.astype(F32), db, ts)
    w_out0_b = w_out0.astype(BF16)
    wr0, br0 = _router_pack(moe0_wg, moe0_bg, moe0_we, moe0_be)
    h, xn, ids, wts = _out_route([(a_o_p, a_o_s), (b_o_p, b_o_s)],
                                 [w_out0_b[:A_HEADS * A_DV], w_out0_b[A_HEADS * A_DV:]], h, ln0_ffn, wr0, br0)
    (h,) = _moe(h, xn, ids, wts, *_expert_weights(moe0_w_gate, moe0_w_up, moe0_w_down))

    c_w = C_HEADS * C_DH
    proj1, c_k_t, c_v_t = _norm_proj(h, ln1_mix, w_in1.astype(BF16), t_cols=((c_w, c_w), (2 * c_w, c_w)),
                                     t_rows=n_p, t_len=tp)
    c_o_p = _stick_prompt(proj1, bp, tp)
    c_o_s = _stick_sample(proj1, n_p, cache_c_k, cache_c_v, page_table, db, ts)
    wr1, br1 = _router_pack(moe1_wg, moe1_bg, moe1_we, moe1_be)
    h, xn, ids, wts = _out_route([(c_o_p, c_o_s)], [w_out1.astype(BF16)], h, ln1_ffn, wr1, br1)
    y_p, y_s = _moe(h, xn, ids, wts, *_expert_weights(moe1_w_gate, moe1_w_up, moe1_w_down), final_gain=ln_f,
                    n_prompt=n_p)

    def heads_last(a_t):
        return jnp.transpose(a_t.reshape(bp, C_HEADS, C_DH, tp), (0, 3, 1, 2))

    y_p, y_s = y_p.reshape(bp, tp, d), y_s.reshape(db, ts, d)
    a_k_p, a_k_s = a_k_rows[:n_p * A_HEADS].reshape(bp, tp, A_HEADS, 2 * A_DH), a_k_rows[n_p * A_HEADS:].reshape(db, ts, A_HEADS, 2 * A_DH)
    a_v_p, a_v_s = a_v_rows[:n_p * A_HEADS].reshape(bp, tp, A_HEADS, A_DV), a_v_rows[n_p * A_HEADS:].reshape(db, ts, A_HEADS, A_DV)
    c_k_p, c_v_p = heads_last(c_k_t), heads_last(c_v_t)
    c_k_s = proj1[n_p:, c_w:2 * c_w].reshape(db, ts, C_HEADS, C_DH)
    c_v_s = proj1[n_p:, 2 * c_w:].reshape(db, ts, C_HEADS, C_DH)
    return (y_p, y_s, a_k_p, a_v_p, b_state_p, c_k_p, c_v_p, a_k_s, a_v_s, b_state_s, c_k_s, c_v_s)
```
